```python
import math
import jax, jax.numpy as jnp
from jax import lax
import numpy as np

D_MODEL = 2048
BATCH = 4
SEQ = 2048
DEPTH = 1

CTX_LEN = 256
GRID_W = 64
MIX_WIDTH = D_MODEL
POOL_WIDTH = MIX_WIDTH // 4
POOL_WINDOWS = (2, 4, 8, 16)
POOL_GROUPS = len(POOL_WINDOWS)
POOL_GROUP_DIM = POOL_WIDTH // POOL_GROUPS
ATTN_WIDTH = MIX_WIDTH - POOL_WIDTH
DIFF_VDIM = 128
DIFF_HEADS = ATTN_WIDTH // DIFF_VDIM
DIFF_QKDIM = DIFF_VDIM // 2
IN_WIDTH = POOL_WIDTH + 3 * ATTN_WIDTH
Q_BLOCK = 128
ROPE_BASE = 10000.0
ROPE_AXIS_DIM = DIFF_QKDIM // 2
PEER_HEADS = 8
PEER_NKEYS = 128
PEER_EXPERTS = PEER_NKEYS * PEER_NKEYS
PEER_QDIM = 256
PEER_HALF = PEER_QDIM // 2
PEER_TOPK = 16
TOKEN_BLOCK = 128
EPS = 1e-6

kernel_name = "hybrid_pool_diffattn_peer_dit_block"


def rmsnorm(x, g):
    xf = x.astype(jnp.float32)
    y = xf * lax.rsqrt(jnp.mean(xf * xf, axis=-1, keepdims=True) + EPS)
    return (y * g.astype(jnp.float32)).astype(x.dtype)


def modulate(h, shift, scale):
    return h * (1 + scale) + shift


def lambda_init(layer_idx):
    return 0.8 - 0.6 * math.exp(-0.3 * layer_idx)


def axial_rope_tables(L):
    n_rows = L // GRID_W
    row = jnp.repeat(jnp.arange(n_rows, dtype=jnp.float32), GRID_W, total_repeat_length=L)
    col = jnp.tile(jnp.arange(GRID_W, dtype=jnp.float32), n_rows)
    half = ROPE_AXIS_DIM // 2
    inv_freq = ROPE_BASE ** (-jnp.arange(half, dtype=jnp.float32) / half)
    ang = jnp.stack([row, col], axis=-1)[:, :, None] * inv_freq
    return jnp.cos(ang), jnp.sin(ang)


def apply_axial_rope(x, cos, sin):
    half = ROPE_AXIS_DIM // 2
    xf = x.astype(jnp.float32).reshape(x.shape[:-1] + (2, 2, half))
    x1, x2 = xf[..., 0, :], xf[..., 1, :]
    c = cos[:, None, None, :, :]
    s = sin[:, None, None, :, :]
    out = jnp.stack([x1 * c - x2 * s, x1 * s + x2 * c], axis=-2)
    return out.reshape(x.shape).astype(x.dtype)


def diff_attend(q, k, v, lam):
    s = jnp.einsum('bqhmd,bkhmd->bhmqk', q, k, preferred_element_type=jnp.float32)
    a = jax.nn.softmax(s * (DIFF_QKDIM ** -0.5), axis=-1)
    a = a[:, :, 0] - lam * a[:, :, 1]
    return jnp.einsum('bhqk,bkhv->bqhv', a.astype(v.dtype), v)


def diff_head_norm(o, subln_g, lam_init):
    B, L = o.shape[0], o.shape[1]
    return (rmsnorm(o, subln_g) * (1.0 - lam_init)).reshape(B, L, ATTN_WIDTH)


def pool_mix(z, pool_w, pool_b, pool_scale):
    B, L = z.shape[0], z.shape[1]
    zg = z.reshape(B, L, POOL_GROUPS, POOL_GROUP_DIM)
    cs = jnp.concatenate([jnp.zeros((B, 1, POOL_GROUPS, POOL_GROUP_DIM), jnp.float32),
                          jnp.cumsum(zg.astype(jnp.float32), axis=1)], axis=1)
    t = jnp.arange(L)
    means = []
    for g, w in enumerate(POOL_WINDOWS):
        lo = jnp.clip(t - w // 2, 0, L)
        hi = jnp.clip(t + w // 2, 0, L)
        total = cs[:, hi, g] - cs[:, lo, g]
        means.append(total / (hi - lo).astype(jnp.float32)[:, None])
    pooled = jnp.stack(means, axis=2)
    y = (pooled - zg.astype(jnp.float32)).astype(z.dtype)
    y = jnp.einsum('blgc,gcd->blgd', y, pool_w) + pool_b
    return y.reshape(B, L, POOL_WIDTH) * pool_scale


def peer(h, peer_wq, peer_keys, peer_u, peer_v):
    B, L, D = h.shape
    hb_all = h.reshape((B * L) // TOKEN_BLOCK, TOKEN_BLOCK, D)

    def block_fn(hb):
        q = (hb @ peer_wq).reshape(TOKEN_BLOCK, PEER_HEADS, 2, PEER_HALF)
        s = jnp.einsum('thpc,hpkc->thpk', q, peer_keys, preferred_element_type=jnp.float32)
        va, ia = lax.top_k(s[:, :, 0], PEER_TOPK)
        vb, ib = lax.top_k(s[:, :, 1], PEER_TOPK)
        cand = (va[..., :, None] + vb[..., None, :]).reshape(TOKEN_BLOCK, PEER_HEADS, PEER_TOPK * PEER_TOPK)
        cand_idx = (ia[..., :, None] * PEER_NKEYS + ib[..., None, :]).reshape(TOKEN_BLOCK, PEER_HEADS, PEER_TOPK * PEER_TOPK)
        top_s, pos = lax.top_k(cand, PEER_TOPK)
        eidx = jnp.take_along_axis(cand_idx, pos, axis=-1)
        gate = jax.nn.softmax(top_s, axis=-1).astype(h.dtype)
        u = peer_u[eidx]
        act = jax.nn.gelu(jnp.einsum('thkd,td->thk', u, hb), approximate=False)
        v = peer_v[eidx]
        return jnp.einsum('thk,thkd->td', gate * act, v)

    return lax.map(block_fn, hb_all).reshape(B, L, D)


def setup_inputs(seed: int = 0) -> dict:
    key = jax.random.key(seed)
    ks = jax.random.split(key, 24)
    f32 = jnp.float32
    nrm = lambda k, shape, s: jax.random.normal(k, shape, f32) * s
    return {
        "x": nrm(ks[0], (BATCH, SEQ, D_MODEL), 1.0),
        "c": nrm(ks[1], (BATCH, D_MODEL), 1.0),
        "ctx": nrm(ks[2], (BATCH, CTX_LEN, D_MODEL), 1.0),
        "c_ctx": nrm(ks[3], (D_MODEL,), 1.0),
        "ada_w": nrm(ks[4], (DEPTH, D_MODEL, 6 * D_MODEL), 0.5 * D_MODEL ** -0.5),
        "ada_b": nrm(ks[5], (DEPTH, 6 * D_MODEL), 0.02),
        "norm1_g": 1.0 + nrm(ks[6], (DEPTH, D_MODEL), 0.02),
        "w_in": nrm(ks[7], (DEPTH, D_MODEL, IN_WIDTH), D_MODEL ** -0.5),
        "pool_w": nrm(ks[8], (DEPTH, POOL_GROUPS, POOL_GROUP_DIM, POOL_GROUP_DIM), POOL_GROUP_DIM ** -0.5),
        "pool_b": nrm(ks[9], (DEPTH, POOL_GROUPS, POOL_GROUP_DIM), 0.02),
        "pool_scale": 1.0 + nrm(ks[10], (DEPTH, POOL_WIDTH), 0.02),
        "diff_lambda": nrm(ks[11], (DEPTH, 4, DIFF_QKDIM), 0.1),
        "subln_g": 1.0 + nrm(ks[12], (DEPTH, DIFF_VDIM), 0.02),
        "w_out": nrm(ks[13], (DEPTH, MIX_WIDTH, D_MODEL), MIX_WIDTH ** -0.5),
        "norm2_g": 1.0 + nrm(ks[14], (DEPTH, D_MODEL), 0.02),
        "peer_wq": nrm(ks[15], (DEPTH, D_MODEL, PEER_HEADS * PEER_QDIM), D_MODEL ** -0.5),
        "peer_keys": nrm(ks[16], (DEPTH, PEER_HEADS, 2, PEER_NKEYS, PEER_HALF), PEER_HALF ** -0.5),
        "peer_u": nrm(ks[17], (DEPTH, PEER_EXPERTS, D_MODEL), D_MODEL ** -0.5),
        "peer_v": nrm(ks[18], (DEPTH, PEER_EXPERTS, D_MODEL), 0.5),
        "final_g": 1.0 + nrm(ks[19], (D_MODEL,), 0.02),
    }


def reference(x, c, ctx, c_ctx, ada_w, ada_b, norm1_g, w_in, pool_w, pool_b, pool_scale,
              diff_lambda, subln_g, w_out, norm2_g, peer_wq, peer_keys, peer_u, peer_v, final_g):
    B, L, D = x.shape
    C = ctx.shape[1]
    n_blocks = L // Q_BLOCK
    cos, sin = axial_rope_tables(L)
    split_pts = [POOL_WIDTH, POOL_WIDTH + ATTN_WIDTH, POOL_WIDTH + 2 * ATTN_WIDTH]

    def project(h, l):
        z_pool, z_q, z_k, z_v = jnp.split(h @ w_in[l], split_pts, axis=-1)
        Bh, Lh = h.shape[0], h.shape[1]
        q = z_q.reshape(Bh, Lh, DIFF_HEADS, 2, DIFF_QKDIM)
        k = z_k.reshape(Bh, Lh, DIFF_HEADS, 2, DIFF_QKDIM)
        v = z_v.reshape(Bh, Lh, DIFF_HEADS, DIFF_VDIM)
        return z_pool, q, k, v

    for l in range(DEPTH):
        last = l == DEPTH - 1
        lam_init = lambda_init(l)
        lq = diff_lambda[l].astype(jnp.float32)
        lam = jnp.exp(jnp.sum(lq[0] * lq[1])) - jnp.exp(jnp.sum(lq[2] * lq[3])) + lam_init

        mod_lat = jax.nn.silu(c) @ ada_w[l] + ada_b[l]
        mod_ctx = jax.nn.silu(c_ctx) @ ada_w[l] + ada_b[l]
        sh1, sc1, g1, sh2, sc2, g2 = [m[:, None, :] for m in jnp.split(mod_lat, 6, axis=-1)]
        csh1, csc1, cg1, csh2, csc2, cg2 = jnp.split(mod_ctx, 6, axis=-1)

        h_lat = modulate(rmsnorm(x, norm1_g[l]), sh1, sc1)
        h_ctx = modulate(rmsnorm(ctx, norm1_g[l]), csh1, csc1)
        zp_lat, q_lat, k_lat, v_lat = project(h_lat, l)
        zp_ctx, q_ctx, k_ctx, v_ctx = project(h_ctx, l)
        q_lat = apply_axial_rope(q_lat, cos, sin)
        k_lat = apply_axial_rope(k_lat, cos, sin)
        k_all = jnp.concatenate([k_ctx, k_lat], axis=1)
        v_all = jnp.concatenate([v_ctx, v_lat], axis=1)

        q_blocks = q_lat.reshape(B, n_blocks, Q_BLOCK, DIFF_HEADS, 2, DIFF_QKDIM).transpose(1, 0, 2, 3, 4, 5)
        o_lat = lax.map(lambda qb: diff_attend(qb, k_all, v_all, lam), q_blocks)
        o_lat = o_lat.transpose(1, 0, 2, 3, 4).reshape(B, L, DIFF_HEADS, DIFF_VDIM)
        attn_lat = diff_head_norm(o_lat, subln_g[l], lam_init)
        pool_lat = pool_mix(zp_lat, pool_w[l], pool_b[l], pool_scale[l])
        x = x + g1 * (jnp.concatenate([pool_lat, attn_lat], axis=-1) @ w_out[l])

        if not last:
            o_ctx = diff_attend(q_ctx, k_ctx, v_ctx, lam)
            attn_ctx = diff_head_norm(o_ctx, subln_g[l], lam_init)
            pool_ctx = pool_mix(zp_ctx, pool_w[l], pool_b[l], pool_scale[l])
            ctx = ctx + cg1 * (jnp.concatenate([pool_ctx, attn_ctx], axis=-1) @ w_out[l])

        f_lat = modulate(rmsnorm(x, norm2_g[l]), sh2, sc2)
        x = x + g2 * peer(f_lat, peer_wq[l], peer_keys[l], peer_u[l], peer_v[l])
        if not last:
            f_ctx = modulate(rmsnorm(ctx, norm2_g[l]), csh2, csc2)
            ctx = ctx + cg2 * peer(f_ctx, peer_wq[l], peer_keys[l], peer_u[l], peer_v[l])

    return rmsnorm(x, final_g)
```

```python
import functools
import math

import jax
import jax.numpy as jnp
import numpy as np
from jax import lax
from jax.experimental import pallas as pl
from jax.experimental.pallas import tpu as pltpu

F32 = jnp.float32
BF16 = jnp.bfloat16

EPS = 1e-6
GRID_W = 64
POOL_WINDOWS = (2, 4, 8, 16)
POOL_GROUP_DIM = 128
POOL_WIDTH = POOL_GROUP_DIM * len(POOL_WINDOWS)
DIFF_VDIM = 128
DIFF_QKDIM = 64
ROPE_BASE = 10000.0
ROPE_AXIS_DIM = 32
ROPE_HALF = ROPE_AXIS_DIM // 2
PEER_HEADS = 8
PEER_NKEYS = 128
PEER_HALF = 128
PEER_TOPK = 16
LANES = 128
POOL_PAD = 8
VMEM_LIMIT_BYTES = 56 * 1024 * 1024
NEG_INF = float("-inf")


def _cparams(sem):
    return pltpu.CompilerParams(dimension_semantics=sem, vmem_limit_bytes=VMEM_LIMIT_BYTES)


def _split_bf16(a):
    hi = a.astype(BF16)
    lo = (a - hi.astype(F32)).astype(BF16)
    return hi, lo


def _ada_kernel(c_ref, w_ref, b_ref, o_ref):
    c = c_ref[...]
    s = c * (1.0 / (1.0 + jnp.exp(-c)))
    o_ref[...] = jnp.dot(s.astype(BF16), w_ref[...].astype(BF16),
                         preferred_element_type=F32) + b_ref[...]


def _ada_call(cc, w, b, tn=1024):
    rows, d = cc.shape
    n = w.shape[1]
    return pl.pallas_call(
        _ada_kernel,
        grid=(n // tn,),
        in_specs=[pl.BlockSpec((rows, d), lambda j: (0, 0)),
                  pl.BlockSpec((d, tn), lambda j: (0, j)),
                  pl.BlockSpec((1, tn), lambda j: (0, j))],
        out_specs=pl.BlockSpec((rows, tn), lambda j: (0, j)),
        out_shape=jax.ShapeDtypeStruct((rows, n), F32),
        compiler_params=_cparams(("arbitrary",)),
        name="ada_mod",
    )(cc, w, b)


def _norm_mod(xf, g, sh, sc):
    ms = jnp.mean(xf * xf, axis=-1, keepdims=True)
    y = xf * lax.rsqrt(ms + EPS) * g
    return y * (1.0 + sc) + sh


def _inproj_kernel(x_ref, g_ref, sh_ref, sc_ref, w_ref, cos_ref, sin_ref, o_ref, h_ref,
                   *, rope, tn):
    n = pl.program_id(2)

    @pl.when(n == 0)
    def _():
        h = _norm_mod(x_ref[0], g_ref[...], sh_ref[0], sc_ref[0])
        h_ref[...] = h.astype(BF16)

    z = jnp.dot(h_ref[...], w_ref[...], preferred_element_type=F32)

    if not rope:
        o_ref[0] = z.astype(BF16)
        return

    q_tiles = (DIFF_VDIM * 12) // tn
    pool_tiles = POOL_WIDTH // tn
    is_rope = jnp.logical_and(n >= pool_tiles, n < pool_tiles + 2 * q_tiles)

    @pl.when(jnp.logical_not(is_rope))
    def _():
        o_ref[0] = z.astype(BF16)

    @pl.when(is_rope)
    def _():
        reps = tn // LANES
        cos = jnp.concatenate([cos_ref[...]] * reps, axis=1)
        sin = jnp.concatenate([sin_ref[...]] * reps, axis=1)
        lane = lax.broadcasted_iota(jnp.int32, z.shape, 1)
        first = (lane % ROPE_AXIS_DIM) < ROPE_HALF
        partner = jnp.where(first, pltpu.roll(z, tn - ROPE_HALF, 1), pltpu.roll(z, ROPE_HALF, 1))
        r = z * cos + partner * sin
        scale = jnp.where(n < pool_tiles + q_tiles, DIFF_QKDIM ** -0.5, 1.0)
        o_ref[0] = (r * scale).astype(BF16)


def _inproj_call(x, g, mod3, mod_row_fn, w, cos, sin, *, rope, col0, ncols, tm=512, tn=512):
    bsz, rows, d = x.shape
    tm = min(tm, rows)
    c0 = col0 // tn
    kern = functools.partial(_inproj_kernel, rope=rope, tn=tn)
    return pl.pallas_call(
        kern,
        grid=(bsz, rows // tm, ncols // tn),
        in_specs=[pl.BlockSpec((1, tm, d), lambda b, i, n: (b, i, 0)),
                  pl.BlockSpec((1, d), lambda b, i, n: (0, 0)),
                  pl.BlockSpec((1, 1, d), lambda b, i, n: (mod_row_fn(b), 0, 0)),
                  pl.BlockSpec((1, 1, d), lambda b, i, n: (mod_row_fn(b), 0, 1)),
                  pl.BlockSpec((d, tn), lambda b, i, n: (0, c0 + n)),
                  pl.BlockSpec((tm, LANES), lambda b, i, n: (i, 0)),
                  pl.BlockSpec((tm, LANES), lambda b, i, n: (i, 0))],
        out_specs=pl.BlockSpec((1, tm, tn), lambda b, i, n: (b, i, n)),
        out_shape=jax.ShapeDtypeStruct((bsz, rows, ncols), BF16),
        scratch_shapes=[pltpu.VMEM((tm, d), BF16)],
        compiler_params=_cparams(("parallel", "parallel", "arbitrary")),
        name="inproj_rope" if rope else "inproj_ctx",
    )(x, g, mod3, mod3, w, cos, sin)


def _attn_kernel(q_ref, kc_ref, vc_ref, kl_ref, vl_ref, dl_ref, sg_ref, o_ref, *, lam_init):
    q = q_ref[0]
    tq = q.shape[0]
    lane = lax.broadcasted_iota(jnp.int32, q.shape, 1)
    zero = jnp.zeros_like(q)
    q2 = jnp.concatenate([jnp.where(lane < DIFF_QKDIM, q, zero),
                          jnp.where(lane >= DIFF_QKDIM, q, zero)], axis=0)
    nt = (((1,), (1,)), ((), ()))
    s_c = lax.dot_general(q2, kc_ref[0], nt, preferred_element_type=F32)
    s_l = lax.dot_general(q2, kl_ref[0], nt, preferred_element_type=F32)
    m = jnp.maximum(jnp.max(s_c, axis=1, keepdims=True), jnp.max(s_l, axis=1, keepdims=True))
    e_c = jnp.exp(s_c - m)
    e_l = jnp.exp(s_l - m)
    den = jnp.sum(e_c, axis=1, keepdims=True) + jnp.sum(e_l, axis=1, keepdims=True)
    o2 = (jnp.dot(e_c.astype(BF16), vc_ref[0], preferred_element_type=F32)
          + jnp.dot(e_l.astype(BF16), vl_ref[0], preferred_element_type=F32))
    o2 = o2 / den
    lq = dl_ref[...]
    lam = (jnp.exp(jnp.sum(lq[0:1] * lq[1:2], axis=1, keepdims=True))
           - jnp.exp(jnp.sum(lq[2:3] * lq[3:4], axis=1, keepdims=True)) + lam_init)
    o = o2[:tq] - lam * o2[tq:]
    ms = jnp.mean(o * o, axis=1, keepdims=True)
    y = o * lax.rsqrt(ms + EPS) * sg_ref[...]
    o_ref[0] = (y * (1.0 - lam_init)).astype(BF16)


def _attn_call(z_lat, kv_ctx, diff_lambda, subln_g, *, lam_init, tq=256):
    bsz, seq, _ = z_lat.shape
    ctx_len = kv_ctx.shape[1]
    heads = 12
    qb = POOL_WIDTH // LANES
    kb = qb + heads
    vb = kb + heads
    kern = functools.partial(_attn_kernel, lam_init=lam_init)
    return pl.pallas_call(
        kern,
        grid=(bsz, heads, seq // tq),
        in_specs=[pl.BlockSpec((1, tq, LANES), lambda b, h, i: (b, i, qb + h)),
                  pl.BlockSpec((1, ctx_len, LANES), lambda b, h, i: (b, 0, h)),
                  pl.BlockSpec((1, ctx_len, LANES), lambda b, h, i: (b, 0, heads + h)),
                  pl.BlockSpec((1, seq, LANES), lambda b, h, i: (b, 0, kb + h)),
                  pl.BlockSpec((1, seq, LANES), lambda b, h, i: (b, 0, vb + h)),
                  pl.BlockSpec(diff_lambda.shape, lambda b, h, i: (0, 0)),
                  pl.BlockSpec((1, LANES), lambda b, h, i: (0, 0))],
        out_specs=pl.BlockSpec((1, tq, LANES), lambda b, h, i: (b, i, h)),
        out_shape=jax.ShapeDtypeStruct((bsz, seq, heads * DIFF_VDIM), BF16),
        compiler_params=_cparams(("parallel", "parallel", "arbitrary")),
        name="diff_attn",
    )(z_lat, kv_ctx, kv_ctx, z_lat, z_lat, diff_lambda, subln_g)


def _pool_kernel(z_ref, w_ref, b_ref, s_ref, o_ref, zp_ref):
    seq = z_ref.shape[1]
    zp_ref[...] = jnp.zeros_like(zp_ref)
    zp_ref[pl.ds(POOL_PAD, seq), :] = z_ref[0].astype(F32)
    t = lax.broadcasted_iota(jnp.int32, (seq, POOL_GROUP_DIM), 0)
    outs = []
    for g, win in enumerate(POOL_WINDOWS):
        half = win // 2
        cols = pl.ds(g * POOL_GROUP_DIM, POOL_GROUP_DIM)
        acc = jnp.zeros((seq, POOL_GROUP_DIM), F32)
        for k in range(-half, half):
            acc = acc + zp_ref[pl.ds(POOL_PAD + k, seq), cols]
        cnt = (jnp.minimum(t + half, seq) - jnp.maximum(t - half, 0)).astype(F32)
        y = acc / cnt - zp_ref[pl.ds(POOL_PAD, seq), cols]
        r = jnp.dot(y.astype(BF16), w_ref[g].astype(BF16), preferred_element_type=F32)
        outs.append((r + b_ref[g]) * s_ref[g])
    o_ref[0] = jnp.concatenate(outs, axis=1).astype(BF16)


def _pool_call(z_lat, pool_w, pool_b, pool_scale):
    bsz, seq, _ = z_lat.shape
    ng = len(POOL_WINDOWS)
    return pl.pallas_call(
        _pool_kernel,
        grid=(bsz,),
        in_specs=[pl.BlockSpec((1, seq, POOL_WIDTH), lambda b: (b, 0, 0)),
                  pl.BlockSpec((ng, POOL_GROUP_DIM, POOL_GROUP_DIM), lambda b: (0, 0, 0)),
                  pl.BlockSpec((ng, 1, POOL_GROUP_DIM), lambda b: (0, 0, 0)),
                  pl.BlockSpec((ng, 1, POOL_GROUP_DIM), lambda b: (0, 0, 0))],
        out_specs=pl.BlockSpec((1, seq, POOL_WIDTH), lambda b: (b, 0, 0)),
        out_shape=jax.ShapeDtypeStruct((bsz, seq, POOL_WIDTH), BF16),
        scratch_shapes=[pltpu.VMEM((seq + 2 * POOL_PAD, POOL_WIDTH), F32)],
        compiler_params=_cparams(("parallel",)),
        name="pool_mix",
    )(z_lat, pool_w, pool_b.reshape(ng, 1, POOL_GROUP_DIM),
      pool_scale.reshape(ng, 1, POOL_GROUP_DIM))


def _outproj_kernel(x_ref, p_ref, a_ref, w_ref, g_ref, o_ref):
    r = (jnp.dot(p_ref[0], w_ref[pl.ds(0, POOL_WIDTH), :], preferred_element_type=F32)
         + jnp.dot(a_ref[0], w_ref[pl.ds(POOL_WIDTH, a_ref.shape[2]), :],
                   preferred_element_type=F32))
    o_ref[0] = x_ref[0] + g_ref[0] * r


def _outproj_call(x, pool_y, attn, w_out, mod3, tm=256):
    bsz, seq, d = x.shape
    aw = attn.shape[2]
    return pl.pallas_call(
        _outproj_kernel,
        grid=(bsz, seq // tm),
        in_specs=[pl.BlockSpec((1, tm, d), lambda b, i: (b, i, 0)),
                  pl.BlockSpec((1, tm, POOL_WIDTH), lambda b, i: (b, i, 0)),
                  pl.BlockSpec((1, tm, aw), lambda b, i: (b, i, 0)),
                  pl.BlockSpec(w_out.shape, lambda b, i: (0, 0)),
                  pl.BlockSpec((1, 1, d), lambda b, i: (b, 0, 2))],
        out_specs=pl.BlockSpec((1, tm, d), lambda b, i: (b, i, 0)),
        out_shape=jax.ShapeDtypeStruct((bsz, seq, d), F32),
        compiler_params=_cparams(("parallel", "parallel")),
        name="outproj_residual",
    )(x, pool_y, attn, w_out, mod3)


def _top16_rows(s):
    n, t = s.shape
    row = lax.broadcasted_iota(jnp.int32, (n, t), 0).astype(F32)
    row16 = lax.broadcasted_iota(jnp.int32, (PEER_TOPK, t), 0)
    work = s
    rank = jnp.full((n, t), 127.0, F32)
    vals = jnp.zeros((PEER_TOPK, t), F32)
    for i in range(PEER_TOPK):
        m = jnp.max(work, axis=0, keepdims=True)
        idx = jnp.min(jnp.where(work == m, row, float(n)), axis=0, keepdims=True)
        sel = row == idx
        rank = jnp.where(sel, float(i), rank)
        vals = jnp.where(row16 == i, m, vals)
        work = jnp.where(sel, NEG_INF, work)
    return vals, rank


_CAND_SLABS = ((0, 0, 8), (0, 8, 8), (1, 0, 8), (2, 0, 5), (3, 0, 4), (4, 0, 3),
               (5, 0, 2), (6, 0, 2), (7, 0, 2))


def _pair_select(va, vb):
    t = va.shape[1]
    sub = lax.broadcasted_iota(jnp.int32, (8, t), 0)
    subf = sub.astype(F32)
    cands, flats = [], []
    for (i, j0, lim) in _CAND_SLABS:
        c = va[i:i + 1, :] + vb[j0:j0 + 8, :]
        cands.append(jnp.where(sub < lim, c, NEG_INF))
        flats.append(subf + float(i * PEER_TOPK + j0))
    cands.append(va[8:16, :] + vb[0:1, :])
    flats.append(subf * float(PEER_TOPK) + float(8 * PEER_TOPK))
    cand = jnp.concatenate(cands, axis=0)
    flat = jnp.concatenate(flats, axis=0)
    work = cand
    big = float(PEER_TOPK * PEER_TOPK)
    for _ in range(PEER_TOPK):
        m = jnp.max(work, axis=0, keepdims=True)
        idx = jnp.min(jnp.where(work == m, flat, big), axis=0, keepdims=True)
        work = jnp.where(flat == idx, NEG_INF, work)
    sel = jnp.logical_and(work == NEG_INF, cand > NEG_INF)
    self32 = sel.astype(F32)
    top = va[0:1, :] + vb[0:1, :]
    z = jnp.sum(jnp.where(sel, jnp.exp(cand - top), 0.0), axis=0, keepdims=True)
    counts = [jnp.sum(self32[0:16], axis=0, keepdims=True)]
    for k in range(2, 9):
        counts.append(jnp.sum(self32[8 * k:8 * k + 8], axis=0, keepdims=True))
    n = jnp.concatenate(counts + [self32[72:80]], axis=0)
    return n, z


def _peer_sel_kernel(x_ref, g_ref, sh_ref, sc_ref, wq_ref, keys_ref,
                     f_ref, na_ref, ea_ref, rb_ref, eb_ref, qt_ref):
    f = _norm_mod(x_ref[0], g_ref[...], sh_ref[0], sc_ref[0]).astype(BF16)
    f_ref[0] = f
    nt = (((1,), (1,)), ((), ()))
    qt_ref[...] = lax.dot_general(wq_ref[...], f, nt, preferred_element_type=F32)

    def scores(hp):
        k_hi, k_lo = _split_bf16(keys_ref[hp])
        q_hi, q_lo = _split_bf16(qt_ref[pl.ds(pl.multiple_of(hp * PEER_HALF, PEER_HALF),
                                              PEER_HALF), :])
        return (jnp.dot(k_hi, q_hi, preferred_element_type=F32)
                + jnp.dot(k_hi, q_lo, preferred_element_type=F32)
                + jnp.dot(k_lo, q_hi, preferred_element_type=F32))

    def head(h, carry):
        sa = scores(2 * h)
        sb = scores(2 * h + 1)
        va, rank_a = _top16_rows(sa)
        vb, rank_b = _top16_rows(sb)
        n, z = _pair_select(va, vb)
        na = jnp.zeros_like(sa)
        for i in range(PEER_TOPK):
            na = jnp.where(rank_a == float(i), n[i:i + 1, :], na)
        na_ref[h] = na
        ea_ref[h] = jnp.exp(sa - va[0:1, :]) / z
        rb_ref[h] = rank_b.astype(BF16)
        eb_ref[h] = jnp.exp(sb - vb[0:1, :]).astype(BF16)
        return carry

    lax.fori_loop(0, PEER_HEADS, head, 0)


def _peer_sel_call(x1, norm_g, mod3, wq_t, keys, ts=256):
    bsz, seq, d = x1.shape
    tokens = bsz * seq
    nblk = seq // ts
    sel_f32 = jax.ShapeDtypeStruct((PEER_HEADS, PEER_NKEYS, tokens), F32)
    sel_bf16 = jax.ShapeDtypeStruct((PEER_HEADS, PEER_NKEYS, tokens), BF16)
    sel_spec = pl.BlockSpec((PEER_HEADS, PEER_NKEYS, ts), lambda b, i: (0, 0, b * nblk + i))
    return pl.pallas_call(
        _peer_sel_kernel,
        grid=(bsz, nblk),
        in_specs=[pl.BlockSpec((1, ts, d), lambda b, i: (b, i, 0)),
                  pl.BlockSpec((1, d), lambda b, i: (0, 0)),
                  pl.BlockSpec((1, 1, d), lambda b, i: (b, 0, 3)),
                  pl.BlockSpec((1, 1, d), lambda b, i: (b, 0, 4)),
                  pl.BlockSpec(wq_t.shape, lambda b, i: (0, 0)),
                  pl.BlockSpec(keys.shape, lambda b, i: (0, 0, 0))],
        out_specs=[pl.BlockSpec((1, ts, d), lambda b, i: (b, i, 0)),
                   sel_spec, sel_spec, sel_spec, sel_spec],
        out_shape=[jax.ShapeDtypeStruct((bsz, seq, d), BF16),
                   sel_f32, sel_f32, sel_bf16, sel_bf16],
        scratch_shapes=[pltpu.VMEM((wq_t.shape[0], ts), F32)],
        compiler_params=_cparams(("parallel", "parallel")),
        name="peer_select",
    )(x1, norm_g, mod3, mod3, wq_t, keys)


def _gelu(x):
    return 0.5 * x * (1.0 + lax.erf(x * (1.0 / math.sqrt(2.0))))


def _peer_kernel(f_ref, u_ref, vt_ref, na_ref, ea_ref, rb_ref, eb_ref, x_ref, g2_ref, fg_ref,
                 o_ref, act_ref, p_ref, acc_ref, *, a_per_step):
    e = pl.program_id(2)
    tm = f_ref.shape[1]

    @pl.when(e == 0)
    def _():
        acc_ref[...] = jnp.zeros_like(acc_ref)

    nt = (((1,), (1,)), ((), ()))
    act_ref[...] = lax.dot_general(u_ref[...], f_ref[0], nt, preferred_element_type=F32)

    def lane_tile(lc, carry):
        cols = pl.ds(pl.multiple_of(lc * LANES, LANES), LANES)
        for a in range(a_per_step):
            rows = pl.ds(a * PEER_NKEYS, PEER_NKEYS)
            g = _gelu(act_ref[rows, cols])
            w = jnp.zeros((PEER_NKEYS, LANES), F32)
            for h in range(PEER_HEADS):
                na = na_ref[h, pl.ds(a, 1), cols]
                ea = ea_ref[h, pl.ds(a, 1), cols]
                rb = rb_ref[h, :, cols].astype(F32)
                eb = eb_ref[h, :, cols].astype(F32)
                w = w + jnp.where(rb < na, eb, 0.0) * ea
            p_ref[rows, cols] = (g * w).astype(BF16)
        return carry

    lax.fori_loop(0, tm // LANES, lane_tile, 0)
    acc_ref[...] += jnp.dot(vt_ref[...], p_ref[...], preferred_element_type=F32)

    @pl.when(e == pl.num_programs(2) - 1)
    def _():
        y = x_ref[0] + g2_ref[0] * acc_ref[...].T
        ms = jnp.mean(y * y, axis=1, keepdims=True)
        o_ref[0] = y * lax.rsqrt(ms + EPS) * fg_ref[...]


def _peer_call(f, u, v_t, na, ea, rb, eb, x1, mod3, final_g, tm=512, ec=1024):
    bsz, seq, d = x1.shape
    experts = u.shape[0]
    nblk = seq // tm
    a_per_step = ec // PEER_NKEYS
    kern = functools.partial(_peer_kernel, a_per_step=a_per_step)
    tok = lambda b, i, e: b * nblk + i
    return pl.pallas_call(
        kern,
        grid=(bsz, nblk, experts // ec),
        in_specs=[pl.BlockSpec((1, tm, d), lambda b, i, e: (b, i, 0)),
                  pl.BlockSpec((ec, d), lambda b, i, e: (e, 0)),
                  pl.BlockSpec((d, ec), lambda b, i, e: (0, e)),
                  pl.BlockSpec((PEER_HEADS, a_per_step, tm), lambda b, i, e: (0, e, tok(b, i, e))),
                  pl.BlockSpec((PEER_HEADS, a_per_step, tm), lambda b, i, e: (0, e, tok(b, i, e))),
                  pl.BlockSpec((PEER_HEADS, PEER_NKEYS, tm), lambda b, i, e: (0, 0, tok(b, i, e))),
                  pl.BlockSpec((PEER_HEADS, PEER_NKEYS, tm), lambda b, i, e: (0, 0, tok(b, i, e))),
                  pl.BlockSpec((1, tm, d), lambda b, i, e: (b, i, 0)),
                  pl.BlockSpec((1, 1, d), lambda b, i, e: (b, 0, 5)),
                  pl.BlockSpec((1, d), lambda b, i, e: (0, 0))],
        out_specs=pl.BlockSpec((1, tm, d), lambda b, i, e: (b, i, 0)),
        out_shape=jax.ShapeDtypeStruct((bsz, seq, d), F32),
        scratch_shapes=[pltpu.VMEM((ec, tm), F32),
                        pltpu.VMEM((ec, tm), BF16),
                        pltpu.VMEM((d, tm), F32)],
        compiler_params=_cparams(("parallel", "parallel", "arbitrary")),
        name="peer_dense",
    )(f, u, v_t, na, ea, rb, eb, x1, mod3, final_g)


def _rope_tables(seq):
    pos = np.arange(seq)
    row = (pos // GRID_W).astype(np.float32)
    col = (pos % GRID_W).astype(np.float32)
    inv_freq = (ROPE_BASE ** (-np.arange(ROPE_HALF, dtype=np.float32) / ROPE_HALF)).astype(np.float32)
    lane = np.arange(LANES)
    axis = (lane % DIFF_QKDIM) // ROPE_AXIS_DIM
    freq = inv_freq[lane % ROPE_HALF]
    p = np.where(axis[None, :] == 0, row[:, None], col[:, None]).astype(np.float32)
    ang = p * freq[None, :]
    sign = np.where((lane % ROPE_AXIS_DIM) < ROPE_HALF, -1.0, 1.0).astype(np.float32)
    return jnp.asarray(np.cos(ang), F32), jnp.asarray(np.sin(ang) * sign[None, :], F32)


def kernel(x, c, ctx, c_ctx, ada_w, ada_b, norm1_g, w_in, pool_w, pool_b, pool_scale,
           diff_lambda, subln_g, w_out, norm2_g, peer_wq, peer_keys, peer_u, peer_v, final_g):
    bsz, seq, d = x.shape
    ctx_len = ctx.shape[1]
    layer = 0
    lam_init = 0.8 - 0.6 * math.exp(-0.3 * layer)
    mod_rows = 8
    assert bsz + 1 <= mod_rows

    cc = jnp.concatenate([c, c_ctx[None, :], jnp.zeros((mod_rows - bsz - 1, d), F32)], axis=0)
    mod = _ada_call(cc, ada_w[layer], ada_b[layer][None, :])
    mod3 = mod.reshape(mod_rows, 1, 6 * d)

    w_in_b = w_in[layer].astype(BF16)
    g1 = norm1_g[layer][None, :]
    cos, sin = _rope_tables(seq)
    in_width = w_in_b.shape[1]
    attn_width = (in_width - POOL_WIDTH) // 3
    z_lat = _inproj_call(x, g1, mod3, lambda b: b, w_in_b, cos, sin,
                         rope=True, col0=0, ncols=in_width)
    kv_ctx = _inproj_call(ctx, g1, mod3, lambda b: bsz, w_in_b, cos, sin,
                          rope=False, col0=POOL_WIDTH + attn_width, ncols=2 * attn_width)

    attn = _attn_call(z_lat, kv_ctx, diff_lambda[layer], subln_g[layer][None, :],
                      lam_init=lam_init)
    pool_y = _pool_call(z_lat, pool_w[layer], pool_b[layer], pool_scale[layer])
    x1 = _outproj_call(x, pool_y, attn, w_out[layer].astype(BF16), mod3)

    wq_t = peer_wq[layer].T.astype(BF16)
    keys = peer_keys[layer].reshape(2 * PEER_HEADS, PEER_NKEYS, PEER_HALF)
    f, na, ea, rb, eb = _peer_sel_call(x1, norm2_g[layer][None, :], mod3, wq_t, keys)
    u_b = peer_u[layer].astype(BF16)
    v_t = peer_v[layer].T.astype(BF16)
    return _peer_call(f, u_b, v_t, na, ea, rb, eb, x1, mod3, final_g[None, :])
```

```python
import functools
import math

import jax
import jax.numpy as jnp
import numpy as np
from jax import lax
from jax.experimental import pallas as pl
from jax.experimental.pallas import tpu as pltpu

F32 = jnp.float32
BF16 = jnp.bfloat16

EPS = 1e-6
GRID_W = 64
POOL_WINDOWS = (2, 4, 8, 16)
POOL_GROUP_DIM = 128
POOL_WIDTH = POOL_GROUP_DIM * len(POOL_WINDOWS)
DIFF_VDIM = 128
DIFF_QKDIM = 64
ROPE_BASE = 10000.0
ROPE_AXIS_DIM = 32
ROPE_HALF = ROPE_AXIS_DIM // 2
PEER_HEADS = 8
PEER_NKEYS = 128
PEER_HALF = 128
PEER_TOPK = 16
LANES = 128
BF16_TILE_LANES = 256
PEER_SUBCHUNKS = 4
POOL_PAD = 8
VMEM_LIMIT_BYTES = 56 * 1024 * 1024
NEG_INF = float("-inf")


def _cparams(sem, flags=None):
    return pltpu.CompilerParams(dimension_semantics=sem, vmem_limit_bytes=VMEM_LIMIT_BYTES,
                                flags=flags)


def _split_bf16(a):
    hi = a.astype(BF16)
    lo = (a - hi.astype(F32)).astype(BF16)
    return hi, lo


def _ada_kernel(c_ref, w_ref, b_ref, o_ref):
    c = c_ref[...]
    s = c * (1.0 / (1.0 + jnp.exp(-c)))
    o_ref[...] = jnp.dot(s.astype(BF16), w_ref[...].astype(BF16),
                         preferred_element_type=F32) + b_ref[...]


def _ada_call(cc, w, b, tn=1024):
    rows, d = cc.shape
    n = w.shape[1]
    return pl.pallas_call(
        _ada_kernel,
        grid=(n // tn,),
        in_specs=[pl.BlockSpec((rows, d), lambda j: (0, 0)),
                  pl.BlockSpec((d, tn), lambda j: (0, j)),
                  pl.BlockSpec((1, tn), lambda j: (0, j))],
        out_specs=pl.BlockSpec((rows, tn), lambda j: (0, j)),
        out_shape=jax.ShapeDtypeStruct((rows, n), F32),
        compiler_params=_cparams(("arbitrary",)),
        name="ada_mod",
    )(cc, w, b)


def _norm_mod(xf, g, sh, sc):
    ms = jnp.mean(xf * xf, axis=-1, keepdims=True)
    y = xf * lax.rsqrt(ms + EPS) * g
    return y * (1.0 + sc) + sh


def _inproj_kernel(x_ref, g_ref, sh_ref, sc_ref, w_ref, cos_ref, sin_ref, o_ref, h_ref, z_ref,
                   *, rope, col0, tn):
    pid = pl.program_id(1)
    n_tiles = o_ref.shape[2] // tn
    q_tiles = (DIFF_VDIM * 12) // tn
    pool_tiles = POOL_WIDTH // tn

    def matmul(n):
        z_ref[n % 2] = jnp.dot(h_ref[...], w_ref[:, pl.ds(col0 + n * tn, tn)],
                               preferred_element_type=F32)

    def epilogue(n):
        z = z_ref[n % 2]
        if rope and pool_tiles <= n < pool_tiles + 2 * q_tiles:
            reps = tn // LANES
            cos = jnp.concatenate([cos_ref[...]] * reps, axis=1)
            sin = jnp.concatenate([sin_ref[...]] * reps, axis=1)
            lane = lax.broadcasted_iota(jnp.int32, z.shape, 1)
            first = (lane % ROPE_AXIS_DIM) < ROPE_HALF
            partner = jnp.where(first, pltpu.roll(z, tn - ROPE_HALF, 1),
                                pltpu.roll(z, ROPE_HALF, 1))
            z = z * cos + partner * sin
            if n < pool_tiles + q_tiles:
                z = z * DIFF_QKDIM ** -0.5
        o_ref[0, :, pl.ds(n * tn, tn)] = z.astype(BF16)

    for stage in range(-1, n_tiles + 1):
        @pl.when(pid >= 0)
        def _(stage=stage):
            if stage < 0:
                h = _norm_mod(x_ref[0], g_ref[...], sh_ref[0], sc_ref[0])
                h_ref[...] = h.astype(BF16)
                return
            if stage < n_tiles:
                matmul(stage)
            if stage >= 1:
                epilogue(stage - 1)


def _inproj_call(x, g, mod3, mod_row_fn, w, cos, sin, *, rope, col0, ncols, tm=512, tn=512):
    bsz, rows, d = x.shape
    tm = min(tm, rows)
    kern = functools.partial(_inproj_kernel, rope=rope, col0=col0, tn=tn)
    return pl.pallas_call(
        kern,
        grid=(bsz, rows // tm),
        in_specs=[pl.BlockSpec((1, tm, d), lambda b, i: (b, i, 0)),
                  pl.BlockSpec((1, d), lambda b, i: (0, 0)),
                  pl.BlockSpec((1, 1, d), lambda b, i: (mod_row_fn(b), 0, 0)),
                  pl.BlockSpec((1, 1, d), lambda b, i: (mod_row_fn(b), 0, 1)),
                  pl.BlockSpec(w.shape, lambda b, i: (0, 0), pipeline_mode=pl.Buffered(1)),
                  pl.BlockSpec((tm, LANES), lambda b, i: (i, 0)),
                  pl.BlockSpec((tm, LANES), lambda b, i: (i, 0))],
        out_specs=pl.BlockSpec((1, tm, ncols), lambda b, i: (b, i, 0)),
        out_shape=jax.ShapeDtypeStruct((bsz, rows, ncols), BF16),
        scratch_shapes=[pltpu.VMEM((tm, d), BF16),
                        pltpu.VMEM((2, tm, tn), F32)],
        compiler_params=_cparams(("parallel", "arbitrary")),
        name="inproj_rope" if rope else "inproj_ctx",
    )(x, g, mod3, mod3, w, cos, sin)


def _attn_kernel(q_ref, kc_ref, vc_ref, kl_ref, vl_ref, dl_ref, sg_ref, o_ref,
                 s_ref, e_ref, den_ref, *, lam_init, sub):
    ctx_len = kc_ref.shape[1]
    seq = kl_ref.shape[1]
    n_sub = q_ref.shape[1] // sub
    pid = pl.program_id(2)
    lq = dl_ref[...]
    lam = (jnp.exp(jnp.sum(lq[0:1] * lq[1:2], axis=1, keepdims=True))
           - jnp.exp(jnp.sum(lq[2:3] * lq[3:4], axis=1, keepdims=True)) + lam_init)
    nt = (((1,), (1,)), ((), ()))
    lane = lax.broadcasted_iota(jnp.int32, (sub, LANES), 1)
    zero = jnp.zeros((sub, LANES), BF16)

    def scores(t):
        q = q_ref[0, pl.ds(t * sub, sub), :]
        q2 = jnp.concatenate([jnp.where(lane < DIFF_QKDIM, q, zero),
                              jnp.where(lane >= DIFF_QKDIM, q, zero)], axis=0)
        s_ref[t % 2, :, pl.ds(0, ctx_len)] = lax.dot_general(
            q2, kc_ref[0], nt, preferred_element_type=F32)
        s_ref[t % 2, :, pl.ds(ctx_len, seq)] = lax.dot_general(
            q2, kl_ref[0], nt, preferred_element_type=F32)

    def numerators(t):
        s = s_ref[t % 2]
        e = jnp.exp(s - jnp.max(s, axis=1, keepdims=True))
        den_ref[t % 2] = jnp.sum(e, axis=1, keepdims=True)
        e_ref[t % 2] = e.astype(BF16)

    def outputs(t):
        o2 = (jnp.dot(e_ref[t % 2, :, pl.ds(0, ctx_len)], vc_ref[0], preferred_element_type=F32)
              + jnp.dot(e_ref[t % 2, :, pl.ds(ctx_len, seq)], vl_ref[0],
                        preferred_element_type=F32))
        o2 = o2 / den_ref[t % 2]
        o = o2[:sub] - lam * o2[sub:]
        ms = jnp.mean(o * o, axis=1, keepdims=True)
        y = o * lax.rsqrt(ms + EPS) * sg_ref[...]
        o_ref[0, pl.ds(t * sub, sub), :] = (y * (1.0 - lam_init)).astype(BF16)

    for stage in range(n_sub + 2):
        @pl.when(pid >= 0)
        def _(stage=stage):
            if stage < n_sub:
                scores(stage)
            if 1 <= stage <= n_sub:
                numerators(stage - 1)
            if stage >= 2:
                outputs(stage - 2)


def _attn_call(z_lat, kv_ctx, diff_lambda, subln_g, *, lam_init, tq=2048, sub=128):
    bsz, seq, _ = z_lat.shape
    ctx_len = kv_ctx.shape[1]
    tq = min(tq, seq)
    heads = 12
    qb = POOL_WIDTH // LANES
    kb = qb + heads
    vb = kb + heads
    kern = functools.partial(_attn_kernel, lam_init=lam_init, sub=sub)
    return pl.pallas_call(
        kern,
        grid=(bsz, heads, seq // tq),
        in_specs=[pl.BlockSpec((1, tq, LANES), lambda b, h, i: (b, i, qb + h)),
                  pl.BlockSpec((1, ctx_len, LANES), lambda b, h, i: (b, 0, h)),
                  pl.BlockSpec((1, ctx_len, LANES), lambda b, h, i: (b, 0, heads + h)),
                  pl.BlockSpec((1, seq, LANES), lambda b, h, i: (b, 0, kb + h)),
                  pl.BlockSpec((1, seq, LANES), lambda b, h, i: (b, 0, vb + h)),
                  pl.BlockSpec(diff_lambda.shape, lambda b, h, i: (0, 0)),
                  pl.BlockSpec((1, LANES), lambda b, h, i: (0, 0))],
        out_specs=pl.BlockSpec((1, tq, LANES), lambda b, h, i: (b, i, h)),
        out_shape=jax.ShapeDtypeStruct((bsz, seq, heads * DIFF_VDIM), BF16),
        scratch_shapes=[pltpu.VMEM((2, 2 * sub, ctx_len + seq), F32),
                        pltpu.VMEM((2, 2 * sub, ctx_len + seq), BF16),
                        pltpu.VMEM((2, 2 * sub, 1), F32)],
        compiler_params=_cparams(("parallel", "parallel", "arbitrary")),
        name="diff_attn",
    )(z_lat, kv_ctx, kv_ctx, z_lat, z_lat, diff_lambda, subln_g)


def _pool_kernel(z_ref, w_ref, b_ref, s_ref, o_ref, zp_ref):
    seq = z_ref.shape[1]
    zp_ref[...] = jnp.zeros_like(zp_ref)
    zp_ref[pl.ds(POOL_PAD, seq), :] = z_ref[0].astype(F32)
    t = lax.broadcasted_iota(jnp.int32, (seq, POOL_GROUP_DIM), 0)
    outs = []
    for g, win in enumerate(POOL_WINDOWS):
        half = win // 2
        cols = pl.ds(g * POOL_GROUP_DIM, POOL_GROUP_DIM)
        acc = jnp.zeros((seq, POOL_GROUP_DIM), F32)
        for k in range(-half, half):
            acc = acc + zp_ref[pl.ds(POOL_PAD + k, seq), cols]
        cnt = (jnp.minimum(t + half, seq) - jnp.maximum(t - half, 0)).astype(F32)
        y = acc / cnt - zp_ref[pl.ds(POOL_PAD, seq), cols]
        r = jnp.dot(y.astype(BF16), w_ref[g].astype(BF16), preferred_element_type=F32)
        outs.append((r + b_ref[g]) * s_ref[g])
    o_ref[0] = jnp.concatenate(outs, axis=1).astype(BF16)


def _pool_call(z_lat, pool_w, pool_b, pool_scale):
    bsz, seq, _ = z_lat.shape
    ng = len(POOL_WINDOWS)
    return pl.pallas_call(
        _pool_kernel,
        grid=(bsz,),
        in_specs=[pl.BlockSpec((1, seq, POOL_WIDTH), lambda b: (b, 0, 0)),
                  pl.BlockSpec((ng, POOL_GROUP_DIM, POOL_GROUP_DIM), lambda b: (0, 0, 0)),
                  pl.BlockSpec((ng, 1, POOL_GROUP_DIM), lambda b: (0, 0, 0)),
                  pl.BlockSpec((ng, 1, POOL_GROUP_DIM), lambda b: (0, 0, 0))],
        out_specs=pl.BlockSpec((1, seq, POOL_WIDTH), lambda b: (b, 0, 0)),
        out_shape=jax.ShapeDtypeStruct((bsz, seq, POOL_WIDTH), BF16),
        scratch_shapes=[pltpu.VMEM((seq + 2 * POOL_PAD, POOL_WIDTH), F32)],
        compiler_params=_cparams(("parallel",)),
        name="pool_mix",
    )(z_lat, pool_w, pool_b.reshape(ng, 1, POOL_GROUP_DIM),
      pool_scale.reshape(ng, 1, POOL_GROUP_DIM))


def _outproj_kernel(x_ref, p_ref, a_ref, w_ref, g_ref, o_ref):
    r = (jnp.dot(p_ref[0], w_ref[pl.ds(0, POOL_WIDTH), :], preferred_element_type=F32)
         + jnp.dot(a_ref[0], w_ref[pl.ds(POOL_WIDTH, a_ref.shape[2]), :],
                   preferred_element_type=F32))
    o_ref[0] = x_ref[0] + g_ref[0] * r


def _outproj_call(x, pool_y, attn, w_out, mod3, tm=256):
    bsz, seq, d = x.shape
    aw = attn.shape[2]
    return pl.pallas_call(
        _outproj_kernel,
        grid=(bsz, seq // tm),
        in_specs=[pl.BlockSpec((1, tm, d), lambda b, i: (b, i, 0)),
                  pl.BlockSpec((1, tm, POOL_WIDTH), lambda b, i: (b, i, 0)),
                  pl.BlockSpec((1, tm, aw), lambda b, i: (b, i, 0)),
                  pl.BlockSpec(w_out.shape, lambda b, i: (0, 0)),
                  pl.BlockSpec((1, 1, d), lambda b, i: (b, 0, 2))],
        out_specs=pl.BlockSpec((1, tm, d), lambda b, i: (b, i, 0)),
        out_shape=jax.ShapeDtypeStruct((bsz, seq, d), F32),
        compiler_params=_cparams(("parallel", "parallel")),
        name="outproj_residual",
    )(x, pool_y, attn, w_out, mod3)


def _top16_rows(s):
    n, t = s.shape
    row = lax.broadcasted_iota(jnp.int32, (n, t), 0).astype(F32)
    row16 = lax.broadcasted_iota(jnp.int32, (PEER_TOPK, t), 0)
    work = s
    rank = jnp.full((n, t), 127.0, F32)
    vals = jnp.zeros((PEER_TOPK, t), F32)
    for i in range(PEER_TOPK):
        m = jnp.max(work, axis=0, keepdims=True)
        idx = jnp.min(jnp.where(work == m, row, float(n)), axis=0, keepdims=True)
        sel = row == idx
        rank = jnp.where(sel, float(i), rank)
        vals = jnp.where(row16 == i, m, vals)
        work = jnp.where(sel, NEG_INF, work)
    return vals, rank


_CAND_SLABS = ((0, 0, 8), (0, 8, 8), (1, 0, 8), (2, 0, 5), (3, 0, 4), (4, 0, 3),
               (5, 0, 2), (6, 0, 2), (7, 0, 2))


def _pair_select(va, vb):
    t = va.shape[1]
    sub = lax.broadcasted_iota(jnp.int32, (8, t), 0)
    subf = sub.astype(F32)
    cands, flats = [], []
    for (i, j0, lim) in _CAND_SLABS:
        c = va[i:i + 1, :] + vb[j0:j0 + 8, :]
        cands.append(jnp.where(sub < lim, c, NEG_INF))
        flats.append(subf + float(i * PEER_TOPK + j0))
    cands.append(va[8:16, :] + vb[0:1, :])
    flats.append(subf * float(PEER_TOPK) + float(8 * PEER_TOPK))
    cand = jnp.concatenate(cands, axis=0)
    flat = jnp.concatenate(flats, axis=0)
    work = cand
    big = float(PEER_TOPK * PEER_TOPK)
    for _ in range(PEER_TOPK):
        m = jnp.max(work, axis=0, keepdims=True)
        idx = jnp.min(jnp.where(work == m, flat, big), axis=0, keepdims=True)
        work = jnp.where(flat == idx, NEG_INF, work)
    sel = jnp.logical_and(work == NEG_INF, cand > NEG_INF)
    self32 = sel.astype(F32)
    top = va[0:1, :] + vb[0:1, :]
    z = jnp.sum(jnp.where(sel, jnp.exp(cand - top), 0.0), axis=0, keepdims=True)
    counts = [jnp.sum(self32[0:16], axis=0, keepdims=True)]
    for k in range(2, 9):
        counts.append(jnp.sum(self32[8 * k:8 * k + 8], axis=0, keepdims=True))
    n = jnp.concatenate(counts + [self32[72:80]], axis=0)
    return n, z


def _peer_sel_kernel(x_ref, g_ref, sh_ref, sc_ref, wq_ref, keys_ref,
                     f_ref, na_ref, ea_ref, rb_ref, eb_ref, qt_ref):
    f = _norm_mod(x_ref[0], g_ref[...], sh_ref[0], sc_ref[0]).astype(BF16)
    f_ref[0] = f
    nt = (((1,), (1,)), ((), ()))
    qt_ref[...] = lax.dot_general(wq_ref[...], f, nt, preferred_element_type=F32)

    def scores(hp):
        k_hi, k_lo = _split_bf16(keys_ref[hp])
        q_hi, q_lo = _split_bf16(qt_ref[pl.ds(pl.multiple_of(hp * PEER_HALF, PEER_HALF),
                                              PEER_HALF), :])
        return (jnp.dot(k_hi, q_hi, preferred_element_type=F32)
                + jnp.dot(k_hi, q_lo, preferred_element_type=F32)
                + jnp.dot(k_lo, q_hi, preferred_element_type=F32))

    def head(h, carry):
        sa = scores(2 * h)
        sb = scores(2 * h + 1)
        va, rank_a = _top16_rows(sa)
        vb, rank_b = _top16_rows(sb)
        n, z = _pair_select(va, vb)
        na = jnp.zeros_like(sa)
        for i in range(PEER_TOPK):
            na = jnp.where(rank_a == float(i), n[i:i + 1, :], na)
        na_ref[h] = na
        ea_ref[h] = jnp.exp(sa - va[0:1, :]) / z
        rb_ref[h] = rank_b.astype(BF16)
        eb_ref[h] = jnp.exp(sb - vb[0:1, :]).astype(BF16)
        return carry

    lax.fori_loop(0, PEER_HEADS, head, 0)


def _peer_sel_call(x1, norm_g, mod3, wq_t, keys, ts=256):
    bsz, seq, d = x1.shape
    tokens = bsz * seq
    nblk = seq // ts
    sel_f32 = jax.ShapeDtypeStruct((PEER_HEADS, PEER_NKEYS, tokens), F32)
    sel_bf16 = jax.ShapeDtypeStruct((PEER_HEADS, PEER_NKEYS, tokens), BF16)
    sel_spec = pl.BlockSpec((PEER_HEADS, PEER_NKEYS, ts), lambda b, i: (0, 0, b * nblk + i))
    return pl.pallas_call(
        _peer_sel_kernel,
        grid=(bsz, nblk),
        in_specs=[pl.BlockSpec((1, ts, d), lambda b, i: (b, i, 0)),
                  pl.BlockSpec((1, d), lambda b, i: (0, 0)),
                  pl.BlockSpec((1, 1, d), lambda b, i: (b, 0, 3)),
                  pl.BlockSpec((1, 1, d), lambda b, i: (b, 0, 4)),
                  pl.BlockSpec(wq_t.shape, lambda b, i: (0, 0)),
                  pl.BlockSpec(keys.shape, lambda b, i: (0, 0, 0))],
        out_specs=[pl.BlockSpec((1, ts, d), lambda b, i: (b, i, 0)),
                   sel_spec, sel_spec, sel_spec, sel_spec],
        out_shape=[jax.ShapeDtypeStruct((bsz, seq, d), BF16),
                   sel_f32, sel_f32, sel_bf16, sel_bf16],
        scratch_shapes=[pltpu.VMEM((wq_t.shape[0], ts), F32)],
        compiler_params=_cparams(("parallel", "parallel")),
        name="peer_select",
    )(x1, norm_g, mod3, mod3, wq_t, keys)


def _gelu(x):
    return 0.5 * x * (1.0 + lax.erf(x * (1.0 / math.sqrt(2.0))))


def _peer_kernel(f_ref, u_ref, vt_ref, na_ref, ea_ref, rb_ref, eb_ref, x_ref, g2_ref, fg_ref,
                 o_ref, act_ref, p_ref, acc_ref, *, a_per_step):
    e = pl.program_id(2)
    tm = f_ref.shape[1]

    @pl.when(e == 0)
    def _():
        acc_ref[...] = jnp.zeros_like(acc_ref)

    nt = (((1,), (1,)), ((), ()))
    sub = u_ref.shape[0] // PEER_SUBCHUNKS
    a_per_sub = a_per_step // PEER_SUBCHUNKS
    zero = jnp.zeros((PEER_NKEYS, BF16_TILE_LANES), BF16)

    def activations(j):
        act_ref[j % 2] = lax.dot_general(u_ref[pl.ds(j * sub, sub), :], f_ref[0], nt,
                                         preferred_element_type=F32)

    def gated(j):
        for lc in range(tm // BF16_TILE_LANES):
            cols = pl.ds(lc * BF16_TILE_LANES, BF16_TILE_LANES)
            for al in range(a_per_sub):
                a = j * a_per_sub + al
                w = zero
                for h in range(PEER_HEADS):
                    na = na_ref[h, pl.ds(a, 1), cols].astype(BF16)
                    ea = ea_ref[h, pl.ds(a, 1), cols].astype(BF16)
                    w = w + jnp.where(rb_ref[h, :, cols] < na, eb_ref[h, :, cols], zero) * ea
                rows = pl.ds(al * PEER_NKEYS, PEER_NKEYS)
                p_ref[j % 2, rows, cols] = _gelu(act_ref[j % 2, rows, cols]).astype(BF16) * w

    def values(j):
        acc_ref[...] += jnp.dot(vt_ref[:, pl.ds(j * sub, sub)], p_ref[j % 2],
                                preferred_element_type=F32)

    for stage in range(PEER_SUBCHUNKS + 2):
        @pl.when(e >= 0)
        def _(stage=stage):
            if stage < PEER_SUBCHUNKS:
                activations(stage)
            if 1 <= stage <= PEER_SUBCHUNKS:
                gated(stage - 1)
            if stage >= 2:
                values(stage - 2)

    @pl.when(e == pl.num_programs(2) - 1)
    def _():
        y = x_ref[0] + g2_ref[0] * acc_ref[...].T
        ms = jnp.mean(y * y, axis=1, keepdims=True)
        o_ref[0] = y * lax.rsqrt(ms + EPS) * fg_ref[...]


def _peer_call(f, u, v_t, na, ea, rb, eb, x1, mod3, final_g, tm=512, ec=1024):
    bsz, seq, d = x1.shape
    experts = u.shape[0]
    nblk = seq // tm
    a_per_step = ec // PEER_NKEYS
    kern = functools.partial(_peer_kernel, a_per_step=a_per_step)
    tok = lambda b, i, e: b * nblk + i
    return pl.pallas_call(
        kern,
        grid=(bsz, nblk, experts // ec),
        in_specs=[pl.BlockSpec((1, tm, d), lambda b, i, e: (b, i, 0)),
                  pl.BlockSpec((ec, d), lambda b, i, e: (e, 0)),
                  pl.BlockSpec((d, ec), lambda b, i, e: (0, e)),
                  pl.BlockSpec((PEER_HEADS, a_per_step, tm), lambda b, i, e: (0, e, tok(b, i, e))),
                  pl.BlockSpec((PEER_HEADS, a_per_step, tm), lambda b, i, e: (0, e, tok(b, i, e))),
                  pl.BlockSpec((PEER_HEADS, PEER_NKEYS, tm), lambda b, i, e: (0, 0, tok(b, i, e))),
                  pl.BlockSpec((PEER_HEADS, PEER_NKEYS, tm), lambda b, i, e: (0, 0, tok(b, i, e))),
                  pl.BlockSpec((1, tm, d), lambda b, i, e: (b, i, 0)),
                  pl.BlockSpec((1, 1, d), lambda b, i, e: (b, 0, 5)),
                  pl.BlockSpec((1, d), lambda b, i, e: (0, 0))],
        out_specs=pl.BlockSpec((1, tm, d), lambda b, i, e: (b, i, 0)),
        out_shape=jax.ShapeDtypeStruct((bsz, seq, d), F32),
        scratch_shapes=[pltpu.VMEM((2, ec // PEER_SUBCHUNKS, tm), F32),
                        pltpu.VMEM((2, ec // PEER_SUBCHUNKS, tm), BF16),
                        pltpu.VMEM((d, tm), F32)],
        compiler_params=_cparams(("parallel", "parallel", "arbitrary")),
        name="peer_dense",
    )(f, u, v_t, na, ea, rb, eb, x1, mod3, final_g)


def _rope_tables(seq):
    pos = np.arange(seq)
    row = (pos // GRID_W).astype(np.float32)
    col = (pos % GRID_W).astype(np.float32)
    inv_freq = (ROPE_BASE ** (-np.arange(ROPE_HALF, dtype=np.float32) / ROPE_HALF)).astype(np.float32)
    lane = np.arange(LANES)
    axis = (lane % DIFF_QKDIM) // ROPE_AXIS_DIM
    freq = inv_freq[lane % ROPE_HALF]
    p = np.where(axis[None, :] == 0, row[:, None], col[:, None]).astype(np.float32)
    ang = p * freq[None, :]
    sign = np.where((lane % ROPE_AXIS_DIM) < ROPE_HALF, -1.0, 1.0).astype(np.float32)
    return jnp.asarray(np.cos(ang), F32), jnp.asarray(np.sin(ang) * sign[None, :], F32)


def kernel(x, c, ctx, c_ctx, ada_w, ada_b, norm1_g, w_in, pool_w, pool_b, pool_scale,
           diff_lambda, subln_g, w_out, norm2_g, peer_wq, peer_keys, peer_u, peer_v, final_g):
    bsz, seq, d = x.shape
    ctx_len = ctx.shape[1]
    layer = 0
    lam_init = 0.8 - 0.6 * math.exp(-0.3 * layer)
    mod_rows = 8
    assert bsz + 1 <= mod_rows

    cc = jnp.concatenate([c, c_ctx[None, :], jnp.zeros((mod_rows - bsz - 1, d), F32)], axis=0)
    mod = _ada_call(cc, ada_w[layer], ada_b[layer][None, :])
    mod3 = mod.reshape(mod_rows, 1, 6 * d)

    w_in_b = w_in[layer].astype(BF16)
    g1 = norm1_g[layer][None, :]
    cos, sin = _rope_tables(seq)
    in_width = w_in_b.shape[1]
    attn_width = (in_width - POOL_WIDTH) // 3
    z_lat = _inproj_call(x, g1, mod3, lambda b: b, w_in_b, cos, sin,
                         rope=True, col0=0, ncols=in_width)
    kv_ctx = _inproj_call(ctx, g1, mod3, lambda b: bsz, w_in_b, cos, sin,
                          rope=False, col0=POOL_WIDTH + attn_width, ncols=2 * attn_width)

    attn = _attn_call(z_lat, kv_ctx, diff_lambda[layer], subln_g[layer][None, :],
                      lam_init=lam_init)
    pool_y = _pool_call(z_lat, pool_w[layer], pool_b[layer], pool_scale[layer])
    x1 = _outproj_call(x, pool_y, attn, w_out[layer].astype(BF16), mod3)

    wq_t = peer_wq[layer].T.astype(BF16)
    keys = peer_keys[layer].reshape(2 * PEER_HEADS, PEER_NKEYS, PEER_HALF)
    f, na, ea, rb, eb = _peer_sel_call(x1, norm2_g[layer][None, :], mod3, wq_t, keys)
    u_b = peer_u[layer].astype(BF16)
    v_t = peer_v[layer].T.astype(BF16)
    return _peer_call(f, u_b, v_t, na, ea, rb, eb, x1, mod3, final_g[None, :])
```

```python
import functools
import math

import jax
import jax.numpy as jnp
import numpy as np
from jax import lax
from jax.experimental import pallas as pl
from jax.experimental.pallas import tpu as pltpu

F32 = jnp.float32
BF16 = jnp.bfloat16

EPS = 1e-6
GRID_W = 64
POOL_WINDOWS = (2, 4, 8, 16)
POOL_GROUP_DIM = 128
POOL_WIDTH = POOL_GROUP_DIM * len(POOL_WINDOWS)
DIFF_VDIM = 128
DIFF_QKDIM = 64
ROPE_BASE = 10000.0
ROPE_AXIS_DIM = 32
ROPE_HALF = ROPE_AXIS_DIM // 2
PEER_HEADS = 8
PEER_NKEYS = 128
PEER_HALF = 128
PEER_TOPK = 16
LANES = 128
BF16_TILE_LANES = 256
SOFTMAX_ROWS = 16
PEER_SUBCHUNKS = 2
POOL_PAD = 8
VMEM_LIMIT_BYTES = 56 * 1024 * 1024
NEG_INF = float("-inf")


def _cparams(sem, flags=None):
    return pltpu.CompilerParams(dimension_semantics=sem, vmem_limit_bytes=VMEM_LIMIT_BYTES,
                                flags=flags)


def _split_bf16(a):
    hi = a.astype(BF16)
    lo = (a - hi.astype(F32)).astype(BF16)
    return hi, lo


def _ada_kernel(c_ref, w_ref, b_ref, o_ref):
    c = c_ref[...]
    s = c * (1.0 / (1.0 + jnp.exp(-c)))
    o_ref[...] = jnp.dot(s.astype(BF16), w_ref[...].astype(BF16),
                         preferred_element_type=F32) + b_ref[...]


def _ada_call(cc, w, b, tn=1024):
    rows, d = cc.shape
    n = w.shape[1]
    return pl.pallas_call(
        _ada_kernel,
        grid=(n // tn,),
        in_specs=[pl.BlockSpec((rows, d), lambda j: (0, 0)),
                  pl.BlockSpec((d, tn), lambda j: (0, j)),
                  pl.BlockSpec((1, tn), lambda j: (0, j))],
        out_specs=pl.BlockSpec((rows, tn), lambda j: (0, j)),
        out_shape=jax.ShapeDtypeStruct((rows, n), F32),
        compiler_params=_cparams(("arbitrary",)),
        name="ada_mod",
    )(cc, w, b)


def _norm_mod(xf, g, sh, sc):
    ms = jnp.mean(xf * xf, axis=-1, keepdims=True)
    y = xf * lax.rsqrt(ms + EPS) * g
    return y * (1.0 + sc) + sh


def _inproj_kernel(x_ref, g_ref, sh_ref, sc_ref, w_ref, cos_ref, sin_ref, o_ref, h_ref, z_ref,
                   *, rope, col0, tn):
    n_tiles = o_ref.shape[2] // tn
    q_tiles = (DIFF_VDIM * 12) // tn
    pool_tiles = POOL_WIDTH // tn

    def matmul(n):
        z_ref[n % 2] = jnp.dot(h_ref[...], w_ref[:, pl.ds(col0 + n * tn, tn)],
                               preferred_element_type=F32)

    def epilogue(n):
        z = z_ref[n % 2]
        if rope and pool_tiles <= n < pool_tiles + 2 * q_tiles:
            reps = tn // LANES
            cos = jnp.concatenate([cos_ref[...]] * reps, axis=1)
            sin = jnp.concatenate([sin_ref[...]] * reps, axis=1)
            lane = lax.broadcasted_iota(jnp.int32, z.shape, 1)
            first = (lane % ROPE_AXIS_DIM) < ROPE_HALF
            partner = jnp.where(first, pltpu.roll(z, tn - ROPE_HALF, 1),
                                pltpu.roll(z, ROPE_HALF, 1))
            z = z * cos + partner * sin
            if n < pool_tiles + q_tiles:
                z = z * DIFF_QKDIM ** -0.5
        o_ref[0, :, pl.ds(n * tn, tn)] = z.astype(BF16)

    h = _norm_mod(x_ref[0], g_ref[...], sh_ref[0], sc_ref[0])
    h_ref[...] = h.astype(BF16)
    for stage in range(n_tiles + 1):
        if stage < n_tiles:
            matmul(stage)
        if stage >= 1:
            epilogue(stage - 1)


def _inproj_call(x, g, mod3, mod_row_fn, w, cos, sin, *, rope, col0, ncols, tm=512, tn=512):
    bsz, rows, d = x.shape
    tm = min(tm, rows)
    kern = functools.partial(_inproj_kernel, rope=rope, col0=col0, tn=tn)
    return pl.pallas_call(
        kern,
        grid=(bsz, rows // tm),
        in_specs=[pl.BlockSpec((1, tm, d), lambda b, i: (b, i, 0)),
                  pl.BlockSpec((1, d), lambda b, i: (0, 0)),
                  pl.BlockSpec((1, 1, d), lambda b, i: (mod_row_fn(b), 0, 0)),
                  pl.BlockSpec((1, 1, d), lambda b, i: (mod_row_fn(b), 0, 1)),
                  pl.BlockSpec(w.shape, lambda b, i: (0, 0), pipeline_mode=pl.Buffered(1)),
                  pl.BlockSpec((tm, LANES), lambda b, i: (i, 0)),
                  pl.BlockSpec((tm, LANES), lambda b, i: (i, 0))],
        out_specs=pl.BlockSpec((1, tm, ncols), lambda b, i: (b, i, 0)),
        out_shape=jax.ShapeDtypeStruct((bsz, rows, ncols), BF16),
        scratch_shapes=[pltpu.VMEM((tm, d), BF16),
                        pltpu.VMEM((2, tm, tn), F32)],
        compiler_params=_cparams(("parallel", "arbitrary")),
        name="inproj_rope" if rope else "inproj_ctx",
    )(x, g, mod3, mod3, w, cos, sin)


def _attn_kernel(q_ref, kc_ref, vc_ref, kl_ref, vl_ref, dl_ref, sg_ref, o_ref,
                 s_ref, e_ref, den_ref, *, lam_init, sub):
    ctx_len = kc_ref.shape[1]
    seq = kl_ref.shape[1]
    n_sub = q_ref.shape[1] // sub
    lq = dl_ref[...]
    lam = (jnp.exp(jnp.sum(lq[0:1] * lq[1:2], axis=1, keepdims=True))
           - jnp.exp(jnp.sum(lq[2:3] * lq[3:4], axis=1, keepdims=True)) + lam_init)
    nt = (((1,), (1,)), ((), ()))
    lane = lax.broadcasted_iota(jnp.int32, (sub, LANES), 1)
    zero = jnp.zeros((sub, LANES), BF16)

    def scores(t):
        q = q_ref[0, pl.ds(t * sub, sub), :]
        q2 = jnp.concatenate([jnp.where(lane < DIFF_QKDIM, q, zero),
                              jnp.where(lane >= DIFF_QKDIM, q, zero)], axis=0)
        s_ref[t % 2, :, pl.ds(0, ctx_len)] = lax.dot_general(
            q2, kc_ref[0], nt, preferred_element_type=F32)
        s_ref[t % 2, :, pl.ds(ctx_len, seq)] = lax.dot_general(
            q2, kl_ref[0], nt, preferred_element_type=F32)

    def numerators(t):
        for r in range(0, 2 * sub, SOFTMAX_ROWS):
            rows = pl.ds(r, SOFTMAX_ROWS)
            s = s_ref[t % 2, rows, :]
            e = jnp.exp(s - jnp.max(s, axis=1, keepdims=True))
            den_ref[t % 2, rows, :] = jnp.sum(e, axis=1, keepdims=True)
            e_ref[t % 2, rows, :] = e.astype(BF16)

    def outputs(t):
        o2 = (jnp.dot(e_ref[t % 2, :, pl.ds(0, ctx_len)], vc_ref[0], preferred_element_type=F32)
              + jnp.dot(e_ref[t % 2, :, pl.ds(ctx_len, seq)], vl_ref[0],
                        preferred_element_type=F32))
        o2 = o2 / den_ref[t % 2]
        o = o2[:sub] - lam * o2[sub:]
        ms = jnp.mean(o * o, axis=1, keepdims=True)
        y = o * lax.rsqrt(ms + EPS) * sg_ref[...]
        o_ref[0, pl.ds(t * sub, sub), :] = (y * (1.0 - lam_init)).astype(BF16)

    for stage in range(n_sub + 2):
        if stage < n_sub:
            scores(stage)
        if 1 <= stage <= n_sub:
            numerators(stage - 1)
        if stage >= 2:
            outputs(stage - 2)


def _attn_call(z_lat, kv_ctx, diff_lambda, subln_g, *, lam_init, tq=2048, sub=128):
    bsz, seq, _ = z_lat.shape
    ctx_len = kv_ctx.shape[1]
    tq = min(tq, seq)
    heads = 12
    qb = POOL_WIDTH // LANES
    kb = qb + heads
    vb = kb + heads
    kern = functools.partial(_attn_kernel, lam_init=lam_init, sub=sub)
    return pl.pallas_call(
        kern,
        grid=(bsz, heads, seq // tq),
        in_specs=[pl.BlockSpec((1, tq, LANES), lambda b, h, i: (b, i, qb + h)),
                  pl.BlockSpec((1, ctx_len, LANES), lambda b, h, i: (b, 0, h)),
                  pl.BlockSpec((1, ctx_len, LANES), lambda b, h, i: (b, 0, heads + h)),
                  pl.BlockSpec((1, seq, LANES), lambda b, h, i: (b, 0, kb + h)),
                  pl.BlockSpec((1, seq, LANES), lambda b, h, i: (b, 0, vb + h)),
                  pl.BlockSpec(diff_lambda.shape, lambda b, h, i: (0, 0)),
                  pl.BlockSpec((1, LANES), lambda b, h, i: (0, 0))],
        out_specs=pl.BlockSpec((1, tq, LANES), lambda b, h, i: (b, i, h)),
        out_shape=jax.ShapeDtypeStruct((bsz, seq, heads * DIFF_VDIM), BF16),
        scratch_shapes=[pltpu.VMEM((2, 2 * sub, ctx_len + seq), F32),
                        pltpu.VMEM((2, 2 * sub, ctx_len + seq), BF16),
                        pltpu.VMEM((2, 2 * sub, 1), F32)],
        compiler_params=_cparams(("parallel", "parallel", "arbitrary")),
        name="diff_attn",
    )(z_lat, kv_ctx, kv_ctx, z_lat, z_lat, diff_lambda, subln_g)


def _pool_kernel(z_ref, w_ref, b_ref, s_ref, o_ref, zp_ref):
    seq = z_ref.shape[1]
    zp_ref[...] = jnp.zeros_like(zp_ref)
    zp_ref[pl.ds(POOL_PAD, seq), :] = z_ref[0].astype(F32)
    t = lax.broadcasted_iota(jnp.int32, (seq, POOL_GROUP_DIM), 0)
    outs = []
    for g, win in enumerate(POOL_WINDOWS):
        half = win // 2
        cols = pl.ds(g * POOL_GROUP_DIM, POOL_GROUP_DIM)
        acc = jnp.zeros((seq, POOL_GROUP_DIM), F32)
        for k in range(-half, half):
            acc = acc + zp_ref[pl.ds(POOL_PAD + k, seq), cols]
        cnt = (jnp.minimum(t + half, seq) - jnp.maximum(t - half, 0)).astype(F32)
        y = acc / cnt - zp_ref[pl.ds(POOL_PAD, seq), cols]
        r = jnp.dot(y.astype(BF16), w_ref[g].astype(BF16), preferred_element_type=F32)
        outs.append((r + b_ref[g]) * s_ref[g])
    o_ref[0] = jnp.concatenate(outs, axis=1).astype(BF16)


def _pool_call(z_lat, pool_w, pool_b, pool_scale):
    bsz, seq, _ = z_lat.shape
    ng = len(POOL_WINDOWS)
    return pl.pallas_call(
        _pool_kernel,
        grid=(bsz,),
        in_specs=[pl.BlockSpec((1, seq, POOL_WIDTH), lambda b: (b, 0, 0)),
                  pl.BlockSpec((ng, POOL_GROUP_DIM, POOL_GROUP_DIM), lambda b: (0, 0, 0)),
                  pl.BlockSpec((ng, 1, POOL_GROUP_DIM), lambda b: (0, 0, 0)),
                  pl.BlockSpec((ng, 1, POOL_GROUP_DIM), lambda b: (0, 0, 0))],
        out_specs=pl.BlockSpec((1, seq, POOL_WIDTH), lambda b: (b, 0, 0)),
        out_shape=jax.ShapeDtypeStruct((bsz, seq, POOL_WIDTH), BF16),
        scratch_shapes=[pltpu.VMEM((seq + 2 * POOL_PAD, POOL_WIDTH), F32)],
        compiler_params=_cparams(("parallel",)),
        name="pool_mix",
    )(z_lat, pool_w, pool_b.reshape(ng, 1, POOL_GROUP_DIM),
      pool_scale.reshape(ng, 1, POOL_GROUP_DIM))


def _outproj_kernel(x_ref, p_ref, a_ref, w_ref, g_ref, o_ref):
    r = (jnp.dot(p_ref[0], w_ref[pl.ds(0, POOL_WIDTH), :], preferred_element_type=F32)
         + jnp.dot(a_ref[0], w_ref[pl.ds(POOL_WIDTH, a_ref.shape[2]), :],
                   preferred_element_type=F32))
    o_ref[0] = x_ref[0] + g_ref[0] * r


def _outproj_call(x, pool_y, attn, w_out, mod3, tm=256):
    bsz, seq, d = x.shape
    aw = attn.shape[2]
    return pl.pallas_call(
        _outproj_kernel,
        grid=(bsz, seq // tm),
        in_specs=[pl.BlockSpec((1, tm, d), lambda b, i: (b, i, 0)),
                  pl.BlockSpec((1, tm, POOL_WIDTH), lambda b, i: (b, i, 0)),
                  pl.BlockSpec((1, tm, aw), lambda b, i: (b, i, 0)),
                  pl.BlockSpec(w_out.shape, lambda b, i: (0, 0)),
                  pl.BlockSpec((1, 1, d), lambda b, i: (b, 0, 2))],
        out_specs=pl.BlockSpec((1, tm, d), lambda b, i: (b, i, 0)),
        out_shape=jax.ShapeDtypeStruct((bsz, seq, d), F32),
        compiler_params=_cparams(("parallel", "parallel")),
        name="outproj_residual",
    )(x, pool_y, attn, w_out, mod3)


def _top16_rows(s):
    n, t = s.shape
    row = lax.broadcasted_iota(jnp.int32, (n, t), 0).astype(F32)
    row16 = lax.broadcasted_iota(jnp.int32, (PEER_TOPK, t), 0)
    work = s
    rank = jnp.full((n, t), 127.0, F32)
    vals = jnp.zeros((PEER_TOPK, t), F32)
    for i in range(PEER_TOPK):
        m = jnp.max(work, axis=0, keepdims=True)
        idx = jnp.min(jnp.where(work == m, row, float(n)), axis=0, keepdims=True)
        sel = row == idx
        rank = jnp.where(sel, float(i), rank)
        vals = jnp.where(row16 == i, m, vals)
        work = jnp.where(sel, NEG_INF, work)
    return vals, rank


_CAND_SLABS = ((0, 0, 8), (0, 8, 8), (1, 0, 8), (2, 0, 5), (3, 0, 4), (4, 0, 3),
               (5, 0, 2), (6, 0, 2), (7, 0, 2))


def _pair_select(va, vb):
    t = va.shape[1]
    sub = lax.broadcasted_iota(jnp.int32, (8, t), 0)
    subf = sub.astype(F32)
    cands, flats = [], []
    for (i, j0, lim) in _CAND_SLABS:
        c = va[i:i + 1, :] + vb[j0:j0 + 8, :]
        cands.append(jnp.where(sub < lim, c, NEG_INF))
        flats.append(subf + float(i * PEER_TOPK + j0))
    cands.append(va[8:16, :] + vb[0:1, :])
    flats.append(subf * float(PEER_TOPK) + float(8 * PEER_TOPK))
    cand = jnp.concatenate(cands, axis=0)
    flat = jnp.concatenate(flats, axis=0)
    work = cand
    big = float(PEER_TOPK * PEER_TOPK)
    for _ in range(PEER_TOPK):
        m = jnp.max(work, axis=0, keepdims=True)
        idx = jnp.min(jnp.where(work == m, flat, big), axis=0, keepdims=True)
        work = jnp.where(flat == idx, NEG_INF, work)
    sel = jnp.logical_and(work == NEG_INF, cand > NEG_INF)
    self32 = sel.astype(F32)
    top = va[0:1, :] + vb[0:1, :]
    z = jnp.sum(jnp.where(sel, jnp.exp(cand - top), 0.0), axis=0, keepdims=True)
    counts = [jnp.sum(self32[0:16], axis=0, keepdims=True)]
    for k in range(2, 9):
        counts.append(jnp.sum(self32[8 * k:8 * k + 8], axis=0, keepdims=True))
    n = jnp.concatenate(counts + [self32[72:80]], axis=0)
    return n, z


def _sort_network(n):
    size = 16
    pairs = []
    p = 1
    while p < size:
        k = p
        while k >= 1:
            for j in range(k % p, size - k, 2 * k):
                for i in range(min(k, size - j - k)):
                    if (i + j) // (2 * p) == (i + j + k) // (2 * p):
                        pairs.append((i + j, i + j + k))
            k //= 2
        p *= 2
    return [(i, j) for (i, j) in pairs if j < n]


def _sorted_top16(slabs):
    s = list(slabs)
    depth = len(s)
    for i, j in _sort_network(depth):
        s[i], s[j] = jnp.maximum(s[i], s[j]), jnp.minimum(s[i], s[j])
    t = s[0].shape[1]
    sub = lax.broadcasted_iota(jnp.int32, (8, t), 0).astype(F32)
    rows = []
    for i in range(PEER_TOPK):
        m = jnp.max(s[0], axis=0, keepdims=True)
        rows.append(m)
        if i == PEER_TOPK - 1:
            break
        first = jnp.min(jnp.where(s[0] == m, sub, 8.0), axis=0, keepdims=True)
        win = sub == first
        for r in range(min(depth, PEER_TOPK - 1 - i)):
            s[r] = jnp.where(win, s[r + 1] if r + 1 < depth else NEG_INF, s[r])
    return rows


def _tie_rows(s, v):
    cnt = jnp.sum(jnp.where(s >= v[PEER_TOPK - 1], 1.0, 0.0), axis=0, keepdims=True)
    tie = jnp.where(cnt != float(PEER_TOPK), 1.0, 0.0)
    for i in range(PEER_TOPK - 1):
        tie = tie + jnp.where(v[i] == v[i + 1], 1.0, 0.0)
    return tie


def _pair_select_fast(va, vb):
    t = va[0].shape[1]
    row16 = lax.broadcasted_iota(jnp.int32, (PEER_TOPK, t), 0)
    sub = lax.broadcasted_iota(jnp.int32, (8, t), 0)
    va_arr = jnp.zeros((PEER_TOPK, t), F32)
    vb_arr = jnp.zeros((PEER_TOPK, t), F32)
    for i in range(PEER_TOPK):
        va_arr = jnp.where(row16 == i, va[i], va_arr)
        vb_arr = jnp.where(row16 == i, vb[i], vb_arr)
    cands = []
    for (i, j0, lim) in _CAND_SLABS:
        cands.append(jnp.where(sub < lim, va[i] + vb_arr[j0:j0 + 8, :], NEG_INF))
    cands.append(va_arr[8:16, :] + vb[0])
    tau = _sorted_top16(cands)[PEER_TOPK - 1]
    top = va[0] + vb[0]
    sel = [jnp.where(c >= tau, 1.0, 0.0) for c in cands]
    z = sum(jnp.sum(s * jnp.exp(c - top), axis=0, keepdims=True) for s, c in zip(sel, cands))
    n = [jnp.sum(sel[0] + sel[1], axis=0, keepdims=True)]
    for k in range(2, 9):
        n.append(jnp.sum(sel[k], axis=0, keepdims=True))
    for r in range(8):
        n.append(sel[9][r:r + 1, :])
    total = sum(n[:8]) + jnp.sum(sel[9], axis=0, keepdims=True)
    return n, z, jnp.where(total != float(PEER_TOPK), 1.0, 0.0)


def _peer_sel_kernel(x_ref, g_ref, sh_ref, sc_ref, wq_ref, keys_ref,
                     f_ref, na_ref, ea_ref, rb_ref, eb_ref, qt_ref):
    f = _norm_mod(x_ref[0], g_ref[...], sh_ref[0], sc_ref[0]).astype(BF16)
    f_ref[0] = f
    nt = (((1,), (1,)), ((), ()))
    qt_ref[...] = lax.dot_general(wq_ref[...], f, nt, preferred_element_type=F32)

    def scores(hp):
        k_hi, k_lo = _split_bf16(keys_ref[hp])
        q_hi, q_lo = _split_bf16(qt_ref[pl.ds(pl.multiple_of(hp * PEER_HALF, PEER_HALF),
                                              PEER_HALF), :])
        return (jnp.dot(k_hi, q_hi, preferred_element_type=F32)
                + jnp.dot(k_hi, q_lo, preferred_element_type=F32)
                + jnp.dot(k_lo, q_hi, preferred_element_type=F32))

    def head(h, carry):
        sa = scores(2 * h)
        sb = scores(2 * h + 1)

        va = _sorted_top16([sa[8 * k:8 * k + 8] for k in range(PEER_NKEYS // 8)])
        vb = _sorted_top16([sb[8 * k:8 * k + 8] for k in range(PEER_NKEYS // 8)])
        n_rows, z, tie2 = _pair_select_fast(va, vb)
        na = jnp.zeros_like(sa)
        rank_b = jnp.zeros_like(sb)
        for i in range(PEER_TOPK):
            na = jnp.where(sa == va[i], n_rows[i], na)
            rank_b = rank_b + jnp.where(vb[i] > sb, 1.0, 0.0)
        na_ref[h] = na
        ea_ref[h] = jnp.exp(sa - va[0]) / z
        rb_ref[h] = rank_b.astype(BF16)
        eb_ref[h] = jnp.exp(sb - vb[0]).astype(BF16)

        tie = tie2 + _tie_rows(sa, va) + _tie_rows(sb, vb)

        @pl.when(jnp.max(tie) > 0.0)
        def _():
            va_t, rank_a_t = _top16_rows(sa)
            vb_t, rank_b_t = _top16_rows(sb)
            n_t, z_t = _pair_select(va_t, vb_t)
            na_t = jnp.zeros_like(sa)
            for i in range(PEER_TOPK):
                na_t = jnp.where(rank_a_t == float(i), n_t[i:i + 1, :], na_t)
            na_ref[h] = na_t
            ea_ref[h] = jnp.exp(sa - va_t[0:1, :]) / z_t
            rb_ref[h] = rank_b_t.astype(BF16)
            eb_ref[h] = jnp.exp(sb - vb_t[0:1, :]).astype(BF16)

        return carry

    lax.fori_loop(0, PEER_HEADS, head, 0)


def _peer_sel_call(x1, norm_g, mod3, wq_t, keys, ts=256):
    bsz, seq, d = x1.shape
    tokens = bsz * seq
    nblk = seq // ts
    sel_f32 = jax.ShapeDtypeStruct((PEER_HEADS, PEER_NKEYS, tokens), F32)
    sel_bf16 = jax.ShapeDtypeStruct((PEER_HEADS, PEER_NKEYS, tokens), BF16)
    sel_spec = pl.BlockSpec((PEER_HEADS, PEER_NKEYS, ts), lambda b, i: (0, 0, b * nblk + i))
    return pl.pallas_call(
        _peer_sel_kernel,
        grid=(bsz, nblk),
        in_specs=[pl.BlockSpec((1, ts, d), lambda b, i: (b, i, 0)),
                  pl.BlockSpec((1, d), lambda b, i: (0, 0)),
                  pl.BlockSpec((1, 1, d), lambda b, i: (b, 0, 3)),
                  pl.BlockSpec((1, 1, d), lambda b, i: (b, 0, 4)),
                  pl.BlockSpec(wq_t.shape, lambda b, i: (0, 0)),
                  pl.BlockSpec(keys.shape, lambda b, i: (0, 0, 0))],
        out_specs=[pl.BlockSpec((1, ts, d), lambda b, i: (b, i, 0)),
                   sel_spec, sel_spec, sel_spec, sel_spec],
        out_shape=[jax.ShapeDtypeStruct((bsz, seq, d), BF16),
                   sel_f32, sel_f32, sel_bf16, sel_bf16],
        scratch_shapes=[pltpu.VMEM((wq_t.shape[0], ts), F32)],
        compiler_params=_cparams(("parallel", "parallel")),
        name="peer_select",
    )(x1, norm_g, mod3, mod3, wq_t, keys)


def _gelu(x):
    return 0.5 * x * (1.0 + lax.erf(x * (1.0 / math.sqrt(2.0))))


def _peer_kernel(f_ref, u_ref, vt_ref, na_ref, ea_ref, rb_ref, eb_ref, x_ref, g2_ref, fg_ref,
                 o_ref, act_ref, p_ref, acc_ref, *, a_per_step):
    e = pl.program_id(2)
    tm = f_ref.shape[1]

    @pl.when(e == 0)
    def _():
        acc_ref[...] = jnp.zeros_like(acc_ref)

    nt = (((1,), (1,)), ((), ()))
    sub = u_ref.shape[0] // PEER_SUBCHUNKS
    a_per_sub = a_per_step // PEER_SUBCHUNKS
    zero = jnp.zeros((PEER_NKEYS, BF16_TILE_LANES), BF16)

    def activations(j):
        act_ref[j % 2] = lax.dot_general(u_ref[pl.ds(j * sub, sub), :], f_ref[0], nt,
                                         preferred_element_type=F32)

    def gated(j):
        for lc in range(tm // BF16_TILE_LANES):
            cols = pl.ds(lc * BF16_TILE_LANES, BF16_TILE_LANES)
            for al in range(a_per_sub):
                a = j * a_per_sub + al
                w = zero
                for h in range(PEER_HEADS):
                    na = na_ref[h, pl.ds(a, 1), cols].astype(BF16)
                    ea = ea_ref[h, pl.ds(a, 1), cols].astype(BF16)
                    w = w + jnp.where(rb_ref[h, :, cols] < na, eb_ref[h, :, cols], zero) * ea
                rows = pl.ds(al * PEER_NKEYS, PEER_NKEYS)
                p_ref[j % 2, rows, cols] = _gelu(act_ref[j % 2, rows, cols]).astype(BF16) * w

    def values(j):
        acc_ref[...] += jnp.dot(vt_ref[:, pl.ds(j * sub, sub)], p_ref[j % 2],
                                preferred_element_type=F32)

    for stage in range(PEER_SUBCHUNKS + 2):
        if stage < PEER_SUBCHUNKS:
            activations(stage)
        if 1 <= stage <= PEER_SUBCHUNKS:
            gated(stage - 1)
        if stage >= 2:
            values(stage - 2)

    @pl.when(e == pl.num_programs(2) - 1)
    def _():
        y = x_ref[0] + g2_ref[0] * acc_ref[...].T
        ms = jnp.mean(y * y, axis=1, keepdims=True)
        o_ref[0] = y * lax.rsqrt(ms + EPS) * fg_ref[...]


def _peer_call(f, u, v_t, na, ea, rb, eb, x1, mod3, final_g, tm=512, ec=1024):
    bsz, seq, d = x1.shape
    experts = u.shape[0]
    nblk = seq // tm
    a_per_step = ec // PEER_NKEYS
    kern = functools.partial(_peer_kernel, a_per_step=a_per_step)
    tok = lambda b, i, e: b * nblk + i
    return pl.pallas_call(
        kern,
        grid=(bsz, nblk, experts // ec),
        in_specs=[pl.BlockSpec((1, tm, d), lambda b, i, e: (b, i, 0)),
                  pl.BlockSpec((ec, d), lambda b, i, e: (e, 0)),
                  pl.BlockSpec((d, ec), lambda b, i, e: (0, e)),
                  pl.BlockSpec((PEER_HEADS, a_per_step, tm), lambda b, i, e: (0, e, tok(b, i, e))),
                  pl.BlockSpec((PEER_HEADS, a_per_step, tm), lambda b, i, e: (0, e, tok(b, i, e))),
                  pl.BlockSpec((PEER_HEADS, PEER_NKEYS, tm), lambda b, i, e: (0, 0, tok(b, i, e))),
                  pl.BlockSpec((PEER_HEADS, PEER_NKEYS, tm), lambda b, i, e: (0, 0, tok(b, i, e))),
                  pl.BlockSpec((1, tm, d), lambda b, i, e: (b, i, 0)),
                  pl.BlockSpec((1, 1, d), lambda b, i, e: (b, 0, 5)),
                  pl.BlockSpec((1, d), lambda b, i, e: (0, 0))],
        out_specs=pl.BlockSpec((1, tm, d), lambda b, i, e: (b, i, 0)),
        out_shape=jax.ShapeDtypeStruct((bsz, seq, d), F32),
        scratch_shapes=[pltpu.VMEM((2, ec // PEER_SUBCHUNKS, tm), F32),
                        pltpu.VMEM((2, ec // PEER_SUBCHUNKS, tm), BF16),
                        pltpu.VMEM((d, tm), F32)],
        compiler_params=_cparams(("parallel", "parallel", "arbitrary")),
        name="peer_dense",
    )(f, u, v_t, na, ea, rb, eb, x1, mod3, final_g)


def _rope_tables(seq):
    pos = np.arange(seq)
    row = (pos // GRID_W).astype(np.float32)
    col = (pos % GRID_W).astype(np.float32)
    inv_freq = (ROPE_BASE ** (-np.arange(ROPE_HALF, dtype=np.float32) / ROPE_HALF)).astype(np.float32)
    lane = np.arange(LANES)
    axis = (lane % DIFF_QKDIM) // ROPE_AXIS_DIM
    freq = inv_freq[lane % ROPE_HALF]
    p = np.where(axis[None, :] == 0, row[:, None], col[:, None]).astype(np.float32)
    ang = p * freq[None, :]
    sign = np.where((lane % ROPE_AXIS_DIM) < ROPE_HALF, -1.0, 1.0).astype(np.float32)
    return jnp.asarray(np.cos(ang), F32), jnp.asarray(np.sin(ang) * sign[None, :], F32)


def kernel(x, c, ctx, c_ctx, ada_w, ada_b, norm1_g, w_in, pool_w, pool_b, pool_scale,
           diff_lambda, subln_g, w_out, norm2_g, peer_wq, peer_keys, peer_u, peer_v, final_g):
    bsz, seq, d = x.shape
    ctx_len = ctx.shape[1]
    layer = 0
    lam_init = 0.8 - 0.6 * math.exp(-0.3 * layer)
    mod_rows = 8
    assert bsz + 1 <= mod_rows

    cc = jnp.concatenate([c, c_ctx[None, :], jnp.zeros((mod_rows - bsz - 1, d), F32)], axis=0)
    mod = _ada_call(cc, ada_w[layer], ada_b[layer][None, :])
    mod3 = mod.reshape(mod_rows, 1, 6 * d)

    w_in_b = w_in[layer].astype(BF16)
    g1 = norm1_g[layer][None, :]
    cos, sin = _rope_tables(seq)
    in_width = w_in_b.shape[1]
    attn_width = (in_width - POOL_WIDTH) // 3
    z_lat = _inproj_call(x, g1, mod3, lambda b: b, w_in_b, cos, sin,
                         rope=True, col0=0, ncols=in_width)
    kv_ctx = _inproj_call(ctx, g1, mod3, lambda b: bsz, w_in_b, cos, sin,
                          rope=False, col0=POOL_WIDTH + attn_width, ncols=2 * attn_width)

    attn = _attn_call(z_lat, kv_ctx, diff_lambda[layer], subln_g[layer][None, :],
                      lam_init=lam_init)
    pool_y = _pool_call(z_lat, pool_w[layer], pool_b[layer], pool_scale[layer])
    x1 = _outproj_call(x, pool_y, attn, w_out[layer].astype(BF16), mod3)

    wq_t = peer_wq[layer].T.astype(BF16)
    keys = peer_keys[layer].reshape(2 * PEER_HEADS, PEER_NKEYS, PEER_HALF)
    f, na, ea, rb, eb = _peer_sel_call(x1, norm2_g[layer][None, :], mod3, wq_t, keys)
    u_b = peer_u[layer].astype(BF16)
    v_t = peer_v[layer].T.astype(BF16)
    return _peer_call(f, u_b, v_t, na, ea, rb, eb, x1, mod3, final_g[None, :])
```

```python
import functools
import math

import jax
import jax.numpy as jnp
import numpy as np
from jax import lax
from jax.experimental import pallas as pl
from jax.experimental.pallas import tpu as pltpu

F32 = jnp.float32
BF16 = jnp.bfloat16

EPS = 1e-6
GRID_W = 64
POOL_WINDOWS = (2, 4, 8, 16)
POOL_GROUP_DIM = 128
POOL_WIDTH = POOL_GROUP_DIM * len(POOL_WINDOWS)
DIFF_VDIM = 128
DIFF_QKDIM = 64
ROPE_BASE = 10000.0
ROPE_AXIS_DIM = 32
ROPE_HALF = ROPE_AXIS_DIM // 2
PEER_HEADS = 8
PEER_NKEYS = 128
PEER_HALF = 128
PEER_TOPK = 16
LANES = 128
BF16_TILE_LANES = 256
SOFTMAX_ROWS = 16
PEER_SUBCHUNKS = 2
POOL_PAD = 8
VMEM_LIMIT_BYTES = 56 * 1024 * 1024
NEG_INF = float("-inf")


def _cparams(sem):
    return pltpu.CompilerParams(dimension_semantics=sem, vmem_limit_bytes=VMEM_LIMIT_BYTES)


def _split_bf16(a):
    hi = a.astype(BF16)
    lo = (a - hi.astype(F32)).astype(BF16)
    return hi, lo


def _ada_kernel(c_ref, w_ref, b_ref, o_ref):
    c = c_ref[...]
    s = c * (1.0 / (1.0 + jnp.exp(-c)))
    o_ref[...] = jnp.dot(s.astype(BF16), w_ref[...].astype(BF16),
                         preferred_element_type=F32) + b_ref[...]


def _ada_call(cc, w, b, tn=1024):
    rows, d = cc.shape
    n = w.shape[1]
    return pl.pallas_call(
        _ada_kernel,
        grid=(n // tn,),
        in_specs=[pl.BlockSpec((rows, d), lambda j: (0, 0)),
                  pl.BlockSpec((d, tn), lambda j: (0, j)),
                  pl.BlockSpec((1, tn), lambda j: (0, j))],
        out_specs=pl.BlockSpec((rows, tn), lambda j: (0, j)),
        out_shape=jax.ShapeDtypeStruct((rows, n), F32),
        compiler_params=_cparams(("arbitrary",)),
        name="ada_mod",
    )(cc, w, b)


def _norm_mod(xf, g, sh, sc):
    ms = jnp.mean(xf * xf, axis=-1, keepdims=True)
    y = xf * lax.rsqrt(ms + EPS) * g
    return y * (1.0 + sc) + sh


def _inproj_kernel(x_ref, g_ref, sh_ref, sc_ref, w_ref, cos_ref, sin_ref, o_ref, h_ref, z_ref,
                   *, rope, col0, tn):
    n_tiles = o_ref.shape[2] // tn
    q_tiles = (DIFF_VDIM * 12) // tn
    pool_tiles = POOL_WIDTH // tn

    def matmul(n):
        z_ref[n % 2] = jnp.dot(h_ref[...], w_ref[:, pl.ds(col0 + n * tn, tn)],
                               preferred_element_type=F32)

    def epilogue(n):
        z = z_ref[n % 2]
        if rope and pool_tiles <= n < pool_tiles + 2 * q_tiles:
            reps = tn // LANES
            cos = jnp.concatenate([cos_ref[...]] * reps, axis=1)
            sin = jnp.concatenate([sin_ref[...]] * reps, axis=1)
            lane = lax.broadcasted_iota(jnp.int32, z.shape, 1)
            first = (lane % ROPE_AXIS_DIM) < ROPE_HALF
            partner = jnp.where(first, pltpu.roll(z, tn - ROPE_HALF, 1),
                                pltpu.roll(z, ROPE_HALF, 1))
            z = z * cos + partner * sin
            if n < pool_tiles + q_tiles:
                z = z * DIFF_QKDIM ** -0.5
        o_ref[0, :, pl.ds(n * tn, tn)] = z.astype(BF16)

    h = _norm_mod(x_ref[0], g_ref[...], sh_ref[0], sc_ref[0])
    h_ref[...] = h.astype(BF16)
    for stage in range(n_tiles + 1):
        if stage < n_tiles:
            matmul(stage)
        if stage >= 1:
            epilogue(stage - 1)


def _inproj_call(x, g, mod3, mod_row_fn, w, cos, sin, *, rope, col0, ncols, tm=512, tn=512):
    bsz, rows, d = x.shape
    tm = min(tm, rows)
    kern = functools.partial(_inproj_kernel, rope=rope, col0=col0, tn=tn)
    return pl.pallas_call(
        kern,
        grid=(bsz, rows // tm),
        in_specs=[pl.BlockSpec((1, tm, d), lambda b, i: (b, i, 0)),
                  pl.BlockSpec((1, d), lambda b, i: (0, 0)),
                  pl.BlockSpec((1, 1, d), lambda b, i: (mod_row_fn(b), 0, 0)),
                  pl.BlockSpec((1, 1, d), lambda b, i: (mod_row_fn(b), 0, 1)),
                  pl.BlockSpec(w.shape, lambda b, i: (0, 0), pipeline_mode=pl.Buffered(1)),
                  pl.BlockSpec((tm, LANES), lambda b, i: (i, 0)),
                  pl.BlockSpec((tm, LANES), lambda b, i: (i, 0))],
        out_specs=pl.BlockSpec((1, tm, ncols), lambda b, i: (b, i, 0)),
        out_shape=jax.ShapeDtypeStruct((bsz, rows, ncols), BF16),
        scratch_shapes=[pltpu.VMEM((tm, d), BF16),
                        pltpu.VMEM((2, tm, tn), F32)],
        compiler_params=_cparams(("parallel", "arbitrary")),
        name="inproj_rope" if rope else "inproj_ctx",
    )(x, g, mod3, mod3, w, cos, sin)


def _attn_kernel(q_ref, kc_ref, vc_ref, kl_ref, vl_ref, dl_ref, sg_ref, o_ref,
                 s_ref, e_ref, va_ref, *, lam_init, sub):
    ctx_len = kc_ref.shape[1]
    seq = kl_ref.shape[1]
    n_sub = q_ref.shape[1] // sub
    lq = dl_ref[...]
    lam = (jnp.exp(jnp.sum(lq[0:1] * lq[1:2], axis=1, keepdims=True))
           - jnp.exp(jnp.sum(lq[2:3] * lq[3:4], axis=1, keepdims=True)) + lam_init)
    nt = (((1,), (1,)), ((), ()))
    lane = lax.broadcasted_iota(jnp.int32, (sub, LANES), 1)
    zero = jnp.zeros((sub, LANES), BF16)

    ones_col = jnp.where(lax.broadcasted_iota(jnp.int32, (ctx_len + seq, LANES), 1) == 0,
                         1.0, 0.0).astype(BF16)
    va_ref[pl.ds(0, ctx_len), pl.ds(0, LANES)] = vc_ref[0]
    va_ref[pl.ds(ctx_len, seq), pl.ds(0, LANES)] = vl_ref[0]
    va_ref[:, pl.ds(LANES, LANES)] = ones_col

    def scores(t):
        q = q_ref[0, pl.ds(t * sub, sub), :]
        q2 = jnp.concatenate([jnp.where(lane < DIFF_QKDIM, q, zero),
                              jnp.where(lane >= DIFF_QKDIM, q, zero)], axis=0)
        s_ref[t % 2, :, pl.ds(0, ctx_len)] = lax.dot_general(
            q2, kc_ref[0], nt, preferred_element_type=F32)
        s_ref[t % 2, :, pl.ds(ctx_len, seq)] = lax.dot_general(
            q2, kl_ref[0], nt, preferred_element_type=F32)

    def numerators(t):
        for r in range(0, 2 * sub, SOFTMAX_ROWS):
            rows = pl.ds(r, SOFTMAX_ROWS)
            s = s_ref[t % 2, rows, :]
            e_ref[t % 2, rows, :] = jnp.exp(s - jnp.max(s, axis=1, keepdims=True)).astype(BF16)

    def outputs(t):
        oa = jnp.dot(e_ref[t % 2], va_ref[...], preferred_element_type=F32)
        o2 = oa[:, :LANES] / oa[:, LANES:LANES + 1]
        o = o2[:sub] - lam * o2[sub:]
        ms = jnp.mean(o * o, axis=1, keepdims=True)
        y = o * lax.rsqrt(ms + EPS) * sg_ref[...]
        o_ref[0, pl.ds(t * sub, sub), :] = (y * (1.0 - lam_init)).astype(BF16)

    for stage in range(n_sub + 2):
        if stage < n_sub:
            scores(stage)
        if 1 <= stage <= n_sub:
            numerators(stage - 1)
        if stage >= 2:
            outputs(stage - 2)


def _attn_call(z_lat, kv_ctx, diff_lambda, subln_g, *, lam_init, tq=2048, sub=128):
    bsz, seq, _ = z_lat.shape
    ctx_len = kv_ctx.shape[1]
    tq = min(tq, seq)
    heads = 12
    qb = POOL_WIDTH // LANES
    kb = qb + heads
    vb = kb + heads
    kern = functools.partial(_attn_kernel, lam_init=lam_init, sub=sub)
    return pl.pallas_call(
        kern,
        grid=(bsz, heads, seq // tq),
        in_specs=[pl.BlockSpec((1, tq, LANES), lambda b, h, i: (b, i, qb + h)),
                  pl.BlockSpec((1, ctx_len, LANES), lambda b, h, i: (b, 0, h)),
                  pl.BlockSpec((1, ctx_len, LANES), lambda b, h, i: (b, 0, heads + h)),
                  pl.BlockSpec((1, seq, LANES), lambda b, h, i: (b, 0, kb + h)),
                  pl.BlockSpec((1, seq, LANES), lambda b, h, i: (b, 0, vb + h)),
                  pl.BlockSpec(diff_lambda.shape, lambda b, h, i: (0, 0)),
                  pl.BlockSpec((1, LANES), lambda b, h, i: (0, 0))],
        out_specs=pl.BlockSpec((1, tq, LANES), lambda b, h, i: (b, i, h)),
        out_shape=jax.ShapeDtypeStruct((bsz, seq, heads * DIFF_VDIM), BF16),
        scratch_shapes=[pltpu.VMEM((2, 2 * sub, ctx_len + seq), F32),
                        pltpu.VMEM((2, 2 * sub, ctx_len + seq), BF16),
                        pltpu.VMEM((ctx_len + seq, 2 * LANES), BF16)],
        compiler_params=_cparams(("parallel", "parallel", "arbitrary")),
        name="diff_attn",
    )(z_lat, kv_ctx, kv_ctx, z_lat, z_lat, diff_lambda, subln_g)


def _pool_kernel(z_ref, w_ref, b_ref, s_ref, o_ref, zp_ref):
    seq = z_ref.shape[1]
    zp_ref[...] = jnp.zeros_like(zp_ref)
    zp_ref[pl.ds(POOL_PAD, seq), :] = z_ref[0].astype(F32)
    t = lax.broadcasted_iota(jnp.int32, (seq, POOL_GROUP_DIM), 0)
    outs = []
    for g, win in enumerate(POOL_WINDOWS):
        half = win // 2
        cols = pl.ds(g * POOL_GROUP_DIM, POOL_GROUP_DIM)
        acc = jnp.zeros((seq, POOL_GROUP_DIM), F32)
        for k in range(-half, half):
            acc = acc + zp_ref[pl.ds(POOL_PAD + k, seq), cols]
        cnt = (jnp.minimum(t + half, seq) - jnp.maximum(t - half, 0)).astype(F32)
        y = acc / cnt - zp_ref[pl.ds(POOL_PAD, seq), cols]
        r = jnp.dot(y.astype(BF16), w_ref[g].astype(BF16), preferred_element_type=F32)
        outs.append((r + b_ref[g]) * s_ref[g])
    o_ref[0] = jnp.concatenate(outs, axis=1).astype(BF16)


def _pool_call(z_lat, pool_w, pool_b, pool_scale):
    bsz, seq, _ = z_lat.shape
    ng = len(POOL_WINDOWS)
    return pl.pallas_call(
        _pool_kernel,
        grid=(bsz,),
        in_specs=[pl.BlockSpec((1, seq, POOL_WIDTH), lambda b: (b, 0, 0)),
                  pl.BlockSpec((ng, POOL_GROUP_DIM, POOL_GROUP_DIM), lambda b: (0, 0, 0)),
                  pl.BlockSpec((ng, 1, POOL_GROUP_DIM), lambda b: (0, 0, 0)),
                  pl.BlockSpec((ng, 1, POOL_GROUP_DIM), lambda b: (0, 0, 0))],
        out_specs=pl.BlockSpec((1, seq, POOL_WIDTH), lambda b: (b, 0, 0)),
        out_shape=jax.ShapeDtypeStruct((bsz, seq, POOL_WIDTH), BF16),
        scratch_shapes=[pltpu.VMEM((seq + 2 * POOL_PAD, POOL_WIDTH), F32)],
        compiler_params=_cparams(("parallel",)),
        name="pool_mix",
    )(z_lat, pool_w, pool_b.reshape(ng, 1, POOL_GROUP_DIM),
      pool_scale.reshape(ng, 1, POOL_GROUP_DIM))


def _outproj_kernel(x_ref, p_ref, a_ref, w_ref, g_ref, u_ref, v_ref, o_ref, ub_ref, vt_ref):
    r = (jnp.dot(p_ref[0], w_ref[pl.ds(0, POOL_WIDTH), :], preferred_element_type=F32)
         + jnp.dot(a_ref[0], w_ref[pl.ds(POOL_WIDTH, a_ref.shape[2]), :],
                   preferred_element_type=F32))
    o_ref[0] = x_ref[0] + g_ref[0] * r
    ub_ref[...] = u_ref[...].astype(BF16)
    vt_ref[...] = v_ref[...].T.astype(BF16)


def _outproj_call(x, pool_y, attn, w_out, mod3, peer_u, peer_v, tm=256):
    bsz, seq, d = x.shape
    aw = attn.shape[2]
    nblk = seq // tm
    experts = peer_u.shape[0]
    urows = experts // (bsz * nblk)
    assert urows * bsz * nblk == experts
    return pl.pallas_call(
        _outproj_kernel,
        grid=(bsz, nblk),
        in_specs=[pl.BlockSpec((1, tm, d), lambda b, i: (b, i, 0)),
                  pl.BlockSpec((1, tm, POOL_WIDTH), lambda b, i: (b, i, 0)),
                  pl.BlockSpec((1, tm, aw), lambda b, i: (b, i, 0)),
                  pl.BlockSpec(w_out.shape, lambda b, i: (0, 0)),
                  pl.BlockSpec((1, 1, d), lambda b, i: (b, 0, 2)),
                  pl.BlockSpec((urows, d), lambda b, i: (b * nblk + i, 0)),
                  pl.BlockSpec((urows, d), lambda b, i: (b * nblk + i, 0))],
        out_specs=[pl.BlockSpec((1, tm, d), lambda b, i: (b, i, 0)),
                   pl.BlockSpec((urows, d), lambda b, i: (b * nblk + i, 0)),
                   pl.BlockSpec((d, urows), lambda b, i: (0, b * nblk + i))],
        out_shape=[jax.ShapeDtypeStruct((bsz, seq, d), F32),
                   jax.ShapeDtypeStruct(peer_u.shape, BF16),
                   jax.ShapeDtypeStruct(peer_v.shape[::-1], BF16)],
        compiler_params=_cparams(("parallel", "parallel")),
        name="outproj_residual",
    )(x, pool_y, attn, w_out, mod3, peer_u, peer_v)


def _top16_rows(s):
    n, t = s.shape
    row = lax.broadcasted_iota(jnp.int32, (n, t), 0).astype(F32)
    row16 = lax.broadcasted_iota(jnp.int32, (PEER_TOPK, t), 0)
    work = s
    rank = jnp.full((n, t), 127.0, F32)
    vals = jnp.zeros((PEER_TOPK, t), F32)
    for i in range(PEER_TOPK):
        m = jnp.max(work, axis=0, keepdims=True)
        idx = jnp.min(jnp.where(work == m, row, float(n)), axis=0, keepdims=True)
        sel = row == idx
        rank = jnp.where(sel, float(i), rank)
        vals = jnp.where(row16 == i, m, vals)
        work = jnp.where(sel, NEG_INF, work)
    return vals, rank


_CAND_SLABS = ((0, 0, 8), (0, 8, 8), (1, 0, 8), (2, 0, 5), (3, 0, 4), (4, 0, 3),
               (5, 0, 2), (6, 0, 2), (7, 0, 2))


def _pair_select(va, vb):
    t = va.shape[1]
    sub = lax.broadcasted_iota(jnp.int32, (8, t), 0)
    subf = sub.astype(F32)
    cands, flats = [], []
    for (i, j0, lim) in _CAND_SLABS:
        c = va[i:i + 1, :] + vb[j0:j0 + 8, :]
        cands.append(jnp.where(sub < lim, c, NEG_INF))
        flats.append(subf + float(i * PEER_TOPK + j0))
    cands.append(va[8:16, :] + vb[0:1, :])
    flats.append(subf * float(PEER_TOPK) + float(8 * PEER_TOPK))
    cand = jnp.concatenate(cands, axis=0)
    flat = jnp.concatenate(flats, axis=0)
    work = cand
    big = float(PEER_TOPK * PEER_TOPK)
    for _ in range(PEER_TOPK):
        m = jnp.max(work, axis=0, keepdims=True)
        idx = jnp.min(jnp.where(work == m, flat, big), axis=0, keepdims=True)
        work = jnp.where(flat == idx, NEG_INF, work)
    sel = jnp.logical_and(work == NEG_INF, cand > NEG_INF)
    self32 = sel.astype(F32)
    top = va[0:1, :] + vb[0:1, :]
    z = jnp.sum(jnp.where(sel, jnp.exp(cand - top), 0.0), axis=0, keepdims=True)
    counts = [jnp.sum(self32[0:16], axis=0, keepdims=True)]
    for k in range(2, 9):
        counts.append(jnp.sum(self32[8 * k:8 * k + 8], axis=0, keepdims=True))
    n = jnp.concatenate(counts + [self32[72:80]], axis=0)
    return n, z


def _sort_network(n):
    size = 16
    pairs = []
    p = 1
    while p < size:
        k = p
        while k >= 1:
            for j in range(k % p, size - k, 2 * k):
                for i in range(min(k, size - j - k)):
                    if (i + j) // (2 * p) == (i + j + k) // (2 * p):
                        pairs.append((i + j, i + j + k))
            k //= 2
        p *= 2
    return [(i, j) for (i, j) in pairs if j < n]


def _sorted_top16(slabs):
    s = list(slabs)
    depth = len(s)
    for i, j in _sort_network(depth):
        s[i], s[j] = jnp.maximum(s[i], s[j]), jnp.minimum(s[i], s[j])
    t = s[0].shape[1]
    sub = lax.broadcasted_iota(jnp.int32, (8, t), 0).astype(F32)
    rows = []
    for i in range(PEER_TOPK):
        m = jnp.max(s[0], axis=0, keepdims=True)
        rows.append(m)
        if i == PEER_TOPK - 1:
            break
        first = jnp.min(jnp.where(s[0] == m, sub, 8.0), axis=0, keepdims=True)
        win = sub == first
        for r in range(min(depth, PEER_TOPK - 1 - i)):
            s[r] = jnp.where(win, s[r + 1] if r + 1 < depth else NEG_INF, s[r])
    return rows


def _tie_rows(s, v):
    cnt = jnp.sum(jnp.where(s >= v[PEER_TOPK - 1], 1.0, 0.0), axis=0, keepdims=True)
    tie = jnp.where(cnt != float(PEER_TOPK), 1.0, 0.0)
    for i in range(PEER_TOPK - 1):
        tie = tie + jnp.where(v[i] == v[i + 1], 1.0, 0.0)
    return tie


def _pair_select_fast(va, vb):
    t = va[0].shape[1]
    row16 = lax.broadcasted_iota(jnp.int32, (PEER_TOPK, t), 0)
    sub = lax.broadcasted_iota(jnp.int32, (8, t), 0)
    va_arr = jnp.zeros((PEER_TOPK, t), F32)
    vb_arr = jnp.zeros((PEER_TOPK, t), F32)
    for i in range(PEER_TOPK):
        va_arr = jnp.where(row16 == i, va[i], va_arr)
        vb_arr = jnp.where(row16 == i, vb[i], vb_arr)
    cands = []
    for (i, j0, lim) in _CAND_SLABS:
        cands.append(jnp.where(sub < lim, va[i] + vb_arr[j0:j0 + 8, :], NEG_INF))
    cands.append(va_arr[8:16, :] + vb[0])
    tau = _sorted_top16(cands)[PEER_TOPK - 1]
    top = va[0] + vb[0]
    sel = [jnp.where(c >= tau, 1.0, 0.0) for c in cands]
    z = sum(jnp.sum(s * jnp.exp(c - top), axis=0, keepdims=True) for s, c in zip(sel, cands))
    n = [jnp.sum(sel[0] + sel[1], axis=0, keepdims=True)]
    for k in range(2, 9):
        n.append(jnp.sum(sel[k], axis=0, keepdims=True))
    for r in range(8):
        n.append(sel[9][r:r + 1, :])
    total = sum(n[:8]) + jnp.sum(sel[9], axis=0, keepdims=True)
    return n, z, jnp.where(total != float(PEER_TOPK), 1.0, 0.0)


def _peer_sel_kernel(x_ref, g_ref, sh_ref, sc_ref, wq_ref, keys_ref,
                     f_ref, na_ref, ea_ref, rb_ref, eb_ref, qt_ref):
    f = _norm_mod(x_ref[0], g_ref[...], sh_ref[0], sc_ref[0]).astype(BF16)
    f_ref[0] = f
    nt = (((1,), (1,)), ((), ()))
    qt_ref[...] = lax.dot_general(wq_ref[...], f, nt, preferred_element_type=F32)

    def scores(hp):
        k_hi, k_lo = _split_bf16(keys_ref[hp])
        q_hi, q_lo = _split_bf16(qt_ref[pl.ds(pl.multiple_of(hp * PEER_HALF, PEER_HALF),
                                              PEER_HALF), :])
        return (jnp.dot(k_hi, q_hi, preferred_element_type=F32)
                + jnp.dot(k_hi, q_lo, preferred_element_type=F32)
                + jnp.dot(k_lo, q_hi, preferred_element_type=F32))

    def head(h, carry):
        sa = scores(2 * h)
        sb = scores(2 * h + 1)

        va = _sorted_top16([sa[8 * k:8 * k + 8] for k in range(PEER_NKEYS // 8)])
        vb = _sorted_top16([sb[8 * k:8 * k + 8] for k in range(PEER_NKEYS // 8)])
        n_rows, z, tie2 = _pair_select_fast(va, vb)
        na = jnp.zeros_like(sa)
        rank_b = jnp.zeros_like(sb)
        for i in range(PEER_TOPK):
            na = jnp.where(sa == va[i], n_rows[i], na)
            rank_b = rank_b + jnp.where(vb[i] > sb, 1.0, 0.0)
        na_ref[h] = na
        ea_ref[h] = jnp.exp(sa - va[0]) / z
        rb_ref[h] = rank_b.astype(BF16)
        eb_ref[h] = jnp.exp(sb - vb[0]).astype(BF16)

        tie = tie2 + _tie_rows(sa, va) + _tie_rows(sb, vb)

        @pl.when(jnp.max(tie) > 0.0)
        def _():
            va_t, rank_a_t = _top16_rows(sa)
            vb_t, rank_b_t = _top16_rows(sb)
            n_t, z_t = _pair_select(va_t, vb_t)
            na_t = jnp.zeros_like(sa)
            for i in range(PEER_TOPK):
                na_t = jnp.where(rank_a_t == float(i), n_t[i:i + 1, :], na_t)
            na_ref[h] = na_t
            ea_ref[h] = jnp.exp(sa - va_t[0:1, :]) / z_t
            rb_ref[h] = rank_b_t.astype(BF16)
            eb_ref[h] = jnp.exp(sb - vb_t[0:1, :]).astype(BF16)

        return carry

    lax.fori_loop(0, PEER_HEADS, head, 0)


def _peer_sel_call(x1, norm_g, mod3, wq_t, keys, ts=512):
    bsz, seq, d = x1.shape
    tokens = bsz * seq
    nblk = seq // ts
    sel_f32 = jax.ShapeDtypeStruct((PEER_HEADS, PEER_NKEYS, tokens), F32)
    sel_bf16 = jax.ShapeDtypeStruct((PEER_HEADS, PEER_NKEYS, tokens), BF16)
    sel_spec = pl.BlockSpec((PEER_HEADS, PEER_NKEYS, ts), lambda b, i: (0, 0, b * nblk + i))
    return pl.pallas_call(
        _peer_sel_kernel,
        grid=(bsz, nblk),
        in_specs=[pl.BlockSpec((1, ts, d), lambda b, i: (b, i, 0)),
                  pl.BlockSpec((1, d), lambda b, i: (0, 0)),
                  pl.BlockSpec((1, 1, d), lambda b, i: (b, 0, 3)),
                  pl.BlockSpec((1, 1, d), lambda b, i: (b, 0, 4)),
                  pl.BlockSpec(wq_t.shape, lambda b, i: (0, 0)),
                  pl.BlockSpec(keys.shape, lambda b, i: (0, 0, 0))],
        out_specs=[pl.BlockSpec((1, ts, d), lambda b, i: (b, i, 0)),
                   sel_spec, sel_spec, sel_spec, sel_spec],
        out_shape=[jax.ShapeDtypeStruct((bsz, seq, d), BF16),
                   sel_f32, sel_f32, sel_bf16, sel_bf16],
        scratch_shapes=[pltpu.VMEM((wq_t.shape[0], ts), F32)],
        compiler_params=_cparams(("parallel", "parallel")),
        name="peer_select",
    )(x1, norm_g, mod3, mod3, wq_t, keys)


def _gelu(x):
    return 0.5 * x * (1.0 + lax.erf(x * (1.0 / math.sqrt(2.0))))


def _peer_kernel(f_ref, u_ref, vt_ref, na_ref, ea_ref, rb_ref, eb_ref, x_ref, g2_ref, fg_ref,
                 o_ref, act_ref, p_ref, acc_ref, *, a_per_step):
    e = pl.program_id(2)
    tm = f_ref.shape[1]

    @pl.when(e == 0)
    def _():
        acc_ref[...] = jnp.zeros_like(acc_ref)

    nt = (((1,), (1,)), ((), ()))
    sub = u_ref.shape[0] // PEER_SUBCHUNKS
    a_per_sub = a_per_step // PEER_SUBCHUNKS
    zero = jnp.zeros((PEER_NKEYS, BF16_TILE_LANES), BF16)

    def activations(j):
        act_ref[j % 2] = lax.dot_general(u_ref[pl.ds(j * sub, sub), :], f_ref[0], nt,
                                         preferred_element_type=F32)

    def gated(j):
        for lc in range(tm // BF16_TILE_LANES):
            cols = pl.ds(lc * BF16_TILE_LANES, BF16_TILE_LANES)
            for al in range(a_per_sub):
                a = j * a_per_sub + al
                w = zero
                for h in range(PEER_HEADS):
                    na = na_ref[h, pl.ds(a, 1), cols].astype(BF16)
                    ea = ea_ref[h, pl.ds(a, 1), cols].astype(BF16)
                    w = w + jnp.where(rb_ref[h, :, cols] < na, eb_ref[h, :, cols], zero) * ea
                rows = pl.ds(al * PEER_NKEYS, PEER_NKEYS)
                p_ref[j % 2, rows, cols] = _gelu(act_ref[j % 2, rows, cols]).astype(BF16) * w

    def values(j):
        acc_ref[...] += jnp.dot(vt_ref[:, pl.ds(j * sub, sub)], p_ref[j % 2],
                                preferred_element_type=F32)

    for stage in range(PEER_SUBCHUNKS + 2):
        if stage < PEER_SUBCHUNKS:
            activations(stage)
        if 1 <= stage <= PEER_SUBCHUNKS:
            gated(stage - 1)
        if stage >= 2:
            values(stage - 2)

    @pl.when(e == pl.num_programs(2) - 1)
    def _():
        y = x_ref[0] + g2_ref[0] * acc_ref[...].T
        ms = jnp.mean(y * y, axis=1, keepdims=True)
        o_ref[0] = y * lax.rsqrt(ms + EPS) * fg_ref[...]


def _peer_call(f, u, v_t, na, ea, rb, eb, x1, mod3, final_g, tm=512, ec=1024):
    bsz, seq, d = x1.shape
    experts = u.shape[0]
    nblk = seq // tm
    a_per_step = ec // PEER_NKEYS
    kern = functools.partial(_peer_kernel, a_per_step=a_per_step)
    tok = lambda b, i, e: b * nblk + i
    return pl.pallas_call(
        kern,
        grid=(bsz, nblk, experts // ec),
        in_specs=[pl.BlockSpec((1, tm, d), lambda b, i, e: (b, i, 0)),
                  pl.BlockSpec((ec, d), lambda b, i, e: (e, 0)),
                  pl.BlockSpec((d, ec), lambda b, i, e: (0, e)),
                  pl.BlockSpec((PEER_HEADS, a_per_step, tm), lambda b, i, e: (0, e, tok(b, i, e))),
                  pl.BlockSpec((PEER_HEADS, a_per_step, tm), lambda b, i, e: (0, e, tok(b, i, e))),
                  pl.BlockSpec((PEER_HEADS, PEER_NKEYS, tm), lambda b, i, e: (0, 0, tok(b, i, e))),
                  pl.BlockSpec((PEER_HEADS, PEER_NKEYS, tm), lambda b, i, e: (0, 0, tok(b, i, e))),
                  pl.BlockSpec((1, tm, d), lambda b, i, e: (b, i, 0)),
                  pl.BlockSpec((1, 1, d), lambda b, i, e: (b, 0, 5)),
                  pl.BlockSpec((1, d), lambda b, i, e: (0, 0))],
        out_specs=pl.BlockSpec((1, tm, d), lambda b, i, e: (b, i, 0)),
        out_shape=jax.ShapeDtypeStruct((bsz, seq, d), F32),
        scratch_shapes=[pltpu.VMEM((2, ec // PEER_SUBCHUNKS, tm), F32),
                        pltpu.VMEM((2, ec // PEER_SUBCHUNKS, tm), BF16),
                        pltpu.VMEM((d, tm), F32)],
        compiler_params=_cparams(("parallel", "parallel", "arbitrary")),
        name="peer_dense",
    )(f, u, v_t, na, ea, rb, eb, x1, mod3, final_g)


def _rope_tables(seq):
    pos = np.arange(seq)
    row = (pos // GRID_W).astype(np.float32)
    col = (pos % GRID_W).astype(np.float32)
    inv_freq = (ROPE_BASE ** (-np.arange(ROPE_HALF, dtype=np.float32) / ROPE_HALF)).astype(np.float32)
    lane = np.arange(LANES)
    axis = (lane % DIFF_QKDIM) // ROPE_AXIS_DIM
    freq = inv_freq[lane % ROPE_HALF]
    p = np.where(axis[None, :] == 0, row[:, None], col[:, None]).astype(np.float32)
    ang = p * freq[None, :]
    sign = np.where((lane % ROPE_AXIS_DIM) < ROPE_HALF, -1.0, 1.0).astype(np.float32)
    return jnp.asarray(np.cos(ang), F32), jnp.asarray(np.sin(ang) * sign[None, :], F32)


def kernel(x, c, ctx, c_ctx, ada_w, ada_b, norm1_g, w_in, pool_w, pool_b, pool_scale,
           diff_lambda, subln_g, w_out, norm2_g, peer_wq, peer_keys, peer_u, peer_v, final_g):
    bsz, seq, d = x.shape
    ctx_len = ctx.shape[1]
    layer = 0
    lam_init = 0.8 - 0.6 * math.exp(-0.3 * layer)
    mod_rows = 8
    assert bsz + 1 <= mod_rows

    cc = jnp.concatenate([c, c_ctx[None, :], jnp.zeros((mod_rows - bsz - 1, d), F32)], axis=0)
    mod = _ada_call(cc, ada_w[layer], ada_b[layer][None, :])
    mod3 = mod.reshape(mod_rows, 1, 6 * d)

    w_in_b = w_in[layer].astype(BF16)
    g1 = norm1_g[layer][None, :]
    cos, sin = _rope_tables(seq)
    in_width = w_in_b.shape[1]
    attn_width = (in_width - POOL_WIDTH) // 3
    z_lat = _inproj_call(x, g1, mod3, lambda b: b, w_in_b, cos, sin,
                         rope=True, col0=0, ncols=in_width)
    kv_ctx = _inproj_call(ctx, g1, mod3, lambda b: bsz, w_in_b, cos, sin,
                          rope=False, col0=POOL_WIDTH + attn_width, ncols=2 * attn_width)

    attn = _attn_call(z_lat, kv_ctx, diff_lambda[layer], subln_g[layer][None, :],
                      lam_init=lam_init)
    pool_y = _pool_call(z_lat, pool_w[layer], pool_b[layer], pool_scale[layer])
    x1, u_b, v_t = _outproj_call(x, pool_y, attn, w_out[layer].astype(BF16), mod3,
                                 peer_u[layer], peer_v[layer])

    wq_t = peer_wq[layer].T.astype(BF16)
    keys = peer_keys[layer].reshape(2 * PEER_HEADS, PEER_NKEYS, PEER_HALF)
    f, na, ea, rb, eb = _peer_sel_call(x1, norm2_g[layer][None, :], mod3, wq_t, keys)
    return _peer_call(f, u_b, v_t, na, ea, rb, eb, x1, mod3, final_g[None, :])
```

```python
import functools
import math

import jax
import jax.numpy as jnp
import numpy as np
from jax import lax
from jax.experimental import pallas as pl
from jax.experimental.pallas import tpu as pltpu

F32 = jnp.float32
BF16 = jnp.bfloat16

EPS = 1e-6
GRID_W = 64
POOL_WINDOWS = (2, 4, 8, 16)
POOL_GROUP_DIM = 128
POOL_WIDTH = POOL_GROUP_DIM * len(POOL_WINDOWS)
DIFF_VDIM = 128
DIFF_QKDIM = 64
ROPE_BASE = 10000.0
ROPE_AXIS_DIM = 32
ROPE_HALF = ROPE_AXIS_DIM // 2
PEER_HEADS = 8
PEER_NKEYS = 128
PEER_HALF = 128
PEER_TOPK = 16
LANES = 128
BF16_TILE_LANES = 256
SOFTMAX_ROWS = 16
PEER_SUBCHUNK_KEYS = (4, 4)
POOL_PAD = 8
VMEM_LIMIT_BYTES = 56 * 1024 * 1024
NEG_INF = float("-inf")


def _cparams(sem):
    return pltpu.CompilerParams(dimension_semantics=sem, vmem_limit_bytes=VMEM_LIMIT_BYTES)


def _split_bf16(a):
    hi = a.astype(BF16)
    lo = (a - hi.astype(F32)).astype(BF16)
    return hi, lo


def _ada_kernel(c_ref, w_ref, b_ref, o_ref):
    c = c_ref[...]
    s = c * (1.0 / (1.0 + jnp.exp(-c)))
    o_ref[...] = jnp.dot(s.astype(BF16), w_ref[...].astype(BF16),
                         preferred_element_type=F32) + b_ref[...]


def _ada_call(cc, w, b, tn=1024):
    rows, d = cc.shape
    n = w.shape[1]
    return pl.pallas_call(
        _ada_kernel,
        grid=(n // tn,),
        in_specs=[pl.BlockSpec((rows, d), lambda j: (0, 0)),
                  pl.BlockSpec((d, tn), lambda j: (0, j)),
                  pl.BlockSpec((1, tn), lambda j: (0, j))],
        out_specs=pl.BlockSpec((rows, tn), lambda j: (0, j)),
        out_shape=jax.ShapeDtypeStruct((rows, n), F32),
        compiler_params=_cparams(("arbitrary",)),
        name="ada_mod",
    )(cc, w, b)


def _norm_mod(xf, g, sh, sc):
    ms = jnp.mean(xf * xf, axis=-1, keepdims=True)
    y = xf * lax.rsqrt(ms + EPS) * g
    return y * (1.0 + sc) + sh


def _inproj_kernel(x_ref, g_ref, sh_ref, sc_ref, w_ref, cos_ref, sin_ref, o_ref, h_ref, z_ref,
                   *, rope, col0, tn):
    n_tiles = o_ref.shape[2] // tn
    q_tiles = (DIFF_VDIM * 12) // tn
    pool_tiles = POOL_WIDTH // tn

    def matmul(n):
        z_ref[n % 2] = jnp.dot(h_ref[...], w_ref[:, pl.ds(col0 + n * tn, tn)],
                               preferred_element_type=F32)

    def epilogue(n):
        z = z_ref[n % 2]
        if rope and pool_tiles <= n < pool_tiles + 2 * q_tiles:
            reps = tn // LANES
            cos = jnp.concatenate([cos_ref[...]] * reps, axis=1)
            sin = jnp.concatenate([sin_ref[...]] * reps, axis=1)
            lane = lax.broadcasted_iota(jnp.int32, z.shape, 1)
            first = (lane % ROPE_AXIS_DIM) < ROPE_HALF
            partner = jnp.where(first, pltpu.roll(z, tn - ROPE_HALF, 1),
                                pltpu.roll(z, ROPE_HALF, 1))
            z = z * cos + partner * sin
            if n < pool_tiles + q_tiles:
                z = z * DIFF_QKDIM ** -0.5
        o_ref[0, :, pl.ds(n * tn, tn)] = z.astype(BF16)

    h = _norm_mod(x_ref[0], g_ref[...], sh_ref[0], sc_ref[0])
    h_ref[...] = h.astype(BF16)
    for stage in range(n_tiles + 1):
        if stage < n_tiles:
            matmul(stage)
        if stage >= 1:
            epilogue(stage - 1)


def _inproj_call(x, g, mod3, mod_row_fn, w, cos, sin, *, rope, col0, ncols, tm=512, tn=512):
    bsz, rows, d = x.shape
    tm = min(tm, rows)
    kern = functools.partial(_inproj_kernel, rope=rope, col0=col0, tn=tn)
    return pl.pallas_call(
        kern,
        grid=(bsz, rows // tm),
        in_specs=[pl.BlockSpec((1, tm, d), lambda b, i: (b, i, 0)),
                  pl.BlockSpec((1, d), lambda b, i: (0, 0)),
                  pl.BlockSpec((1, 1, d), lambda b, i: (mod_row_fn(b), 0, 0)),
                  pl.BlockSpec((1, 1, d), lambda b, i: (mod_row_fn(b), 0, 1)),
                  pl.BlockSpec(w.shape, lambda b, i: (0, 0), pipeline_mode=pl.Buffered(1)),
                  pl.BlockSpec((tm, LANES), lambda b, i: (i, 0)),
                  pl.BlockSpec((tm, LANES), lambda b, i: (i, 0))],
        out_specs=pl.BlockSpec((1, tm, ncols), lambda b, i: (b, i, 0)),
        out_shape=jax.ShapeDtypeStruct((bsz, rows, ncols), BF16),
        scratch_shapes=[pltpu.VMEM((tm, d), BF16),
                        pltpu.VMEM((2, tm, tn), F32)],
        compiler_params=_cparams(("parallel", "arbitrary")),
        name="inproj_rope" if rope else "inproj_ctx",
    )(x, g, mod3, mod3, w, cos, sin)


def _attn_kernel(q_ref, kc_ref, vc_ref, kl_ref, vl_ref, dl_ref, sg_ref, u_ref, v_ref,
                 o_ref, ub_ref, vt_ref, s_ref, e_ref, va_ref, *, lam_init, sub, side_heads):
    ctx_len = kc_ref.shape[1]
    seq = kl_ref.shape[1]
    n_sub = q_ref.shape[1] // sub
    lq = dl_ref[...]
    lam = (jnp.exp(jnp.sum(lq[0:1] * lq[1:2], axis=1, keepdims=True))
           - jnp.exp(jnp.sum(lq[2:3] * lq[3:4], axis=1, keepdims=True)) + lam_init)
    nt = (((1,), (1,)), ((), ()))
    lane = lax.broadcasted_iota(jnp.int32, (sub, LANES), 1)
    zero = jnp.zeros((sub, LANES), BF16)

    ones_col = jnp.where(lax.broadcasted_iota(jnp.int32, (ctx_len + seq, LANES), 1) == 0,
                         1.0, 0.0).astype(BF16)
    va_ref[pl.ds(0, ctx_len), pl.ds(0, LANES)] = vc_ref[0]
    va_ref[pl.ds(ctx_len, seq), pl.ds(0, LANES)] = vl_ref[0]
    va_ref[:, pl.ds(LANES, LANES)] = ones_col

    def scores(t):
        q = q_ref[0, pl.ds(t * sub, sub), :]
        q2 = jnp.concatenate([jnp.where(lane < DIFF_QKDIM, q, zero),
                              jnp.where(lane >= DIFF_QKDIM, q, zero)], axis=0)
        s_ref[t % 2, :, pl.ds(0, ctx_len)] = lax.dot_general(
            q2, kc_ref[0], nt, preferred_element_type=F32)
        s_ref[t % 2, :, pl.ds(ctx_len, seq)] = lax.dot_general(
            q2, kl_ref[0], nt, preferred_element_type=F32)

    def numerators(t):
        for r in range(0, 2 * sub, SOFTMAX_ROWS):
            rows = pl.ds(r, SOFTMAX_ROWS)
            s = s_ref[t % 2, rows, :]
            e_ref[t % 2, rows, :] = jnp.exp(s - jnp.max(s, axis=1, keepdims=True)).astype(BF16)

    def outputs(t):
        oa = jnp.dot(e_ref[t % 2], va_ref[...], preferred_element_type=F32)
        o2 = oa[:, :LANES] / oa[:, LANES:LANES + 1]
        o = o2[:sub] - lam * o2[sub:]
        ms = jnp.mean(o * o, axis=1, keepdims=True)
        y = o * lax.rsqrt(ms + EPS) * sg_ref[...]
        o_ref[0, pl.ds(t * sub, sub), :] = (y * (1.0 - lam_init)).astype(BF16)

    for stage in range(n_sub + 2):
        if stage < n_sub:
            scores(stage)
        if 1 <= stage <= n_sub:
            numerators(stage - 1)
        if stage >= 2:
            outputs(stage - 2)

    @pl.when(jnp.logical_and(pl.program_id(1) < side_heads, pl.program_id(2) == 0))
    def _():
        ub_ref[...] = u_ref[...].astype(BF16)
        vt_ref[...] = v_ref[...].T.astype(BF16)


def _attn_call(z_lat, kv_ctx, diff_lambda, subln_g, peer_u, peer_v, *, lam_init, tq=2048,
               sub=128):
    bsz, seq, _ = z_lat.shape
    ctx_len = kv_ctx.shape[1]
    tq = min(tq, seq)
    heads = 12
    qb = POOL_WIDTH // LANES
    kb = qb + heads
    vb = kb + heads
    experts, d = peer_u.shape
    side_heads = 8
    trows = experts // (bsz * side_heads)
    assert trows * bsz * side_heads == experts
    tblk = lambda b, h, i: b * side_heads + jnp.minimum(h, side_heads - 1)
    kern = functools.partial(_attn_kernel, lam_init=lam_init, sub=sub, side_heads=side_heads)
    return pl.pallas_call(
        kern,
        grid=(bsz, heads, seq // tq),
        in_specs=[pl.BlockSpec((1, tq, LANES), lambda b, h, i: (b, i, qb + h)),
                  pl.BlockSpec((1, ctx_len, LANES), lambda b, h, i: (b, 0, h)),
                  pl.BlockSpec((1, ctx_len, LANES), lambda b, h, i: (b, 0, heads + h)),
                  pl.BlockSpec((1, seq, LANES), lambda b, h, i: (b, 0, kb + h)),
                  pl.BlockSpec((1, seq, LANES), lambda b, h, i: (b, 0, vb + h)),
                  pl.BlockSpec(diff_lambda.shape, lambda b, h, i: (0, 0)),
                  pl.BlockSpec((1, LANES), lambda b, h, i: (0, 0)),
                  pl.BlockSpec((trows, d), lambda b, h, i: (tblk(b, h, i), 0)),
                  pl.BlockSpec((trows, d), lambda b, h, i: (tblk(b, h, i), 0))],
        out_specs=[pl.BlockSpec((1, tq, LANES), lambda b, h, i: (b, i, h)),
                   pl.BlockSpec((trows, d), lambda b, h, i: (tblk(b, h, i), 0)),
                   pl.BlockSpec((d, trows), lambda b, h, i: (0, tblk(b, h, i)))],
        out_shape=[jax.ShapeDtypeStruct((bsz, seq, heads * DIFF_VDIM), BF16),
                   jax.ShapeDtypeStruct((experts, d), BF16),
                   jax.ShapeDtypeStruct((d, experts), BF16)],
        scratch_shapes=[pltpu.VMEM((2, 2 * sub, ctx_len + seq), F32),
                        pltpu.VMEM((2, 2 * sub, ctx_len + seq), BF16),
                        pltpu.VMEM((ctx_len + seq, 2 * LANES), BF16)],
        compiler_params=_cparams(("parallel", "arbitrary", "arbitrary")),
        name="diff_attn",
    )(z_lat, kv_ctx, kv_ctx, z_lat, z_lat, diff_lambda, subln_g, peer_u, peer_v)


def _pool_kernel(z_ref, w_ref, b_ref, s_ref, o_ref, zp_ref):
    seq = z_ref.shape[1]
    zp_ref[...] = jnp.zeros_like(zp_ref)
    zp_ref[pl.ds(POOL_PAD, seq), :] = z_ref[0].astype(F32)
    t = lax.broadcasted_iota(jnp.int32, (seq, POOL_GROUP_DIM), 0)
    outs = []
    for g, win in enumerate(POOL_WINDOWS):
        half = win // 2
        cols = pl.ds(g * POOL_GROUP_DIM, POOL_GROUP_DIM)
        acc = jnp.zeros((seq, POOL_GROUP_DIM), F32)
        for k in range(-half, half):
            acc = acc + zp_ref[pl.ds(POOL_PAD + k, seq), cols]
        cnt = (jnp.minimum(t + half, seq) - jnp.maximum(t - half, 0)).astype(F32)
        y = acc / cnt - zp_ref[pl.ds(POOL_PAD, seq), cols]
        r = jnp.dot(y.astype(BF16), w_ref[g].astype(BF16), preferred_element_type=F32)
        outs.append((r + b_ref[g]) * s_ref[g])
    o_ref[0] = jnp.concatenate(outs, axis=1).astype(BF16)


def _pool_call(z_lat, pool_w, pool_b, pool_scale):
    bsz, seq, _ = z_lat.shape
    ng = len(POOL_WINDOWS)
    return pl.pallas_call(
        _pool_kernel,
        grid=(bsz,),
        in_specs=[pl.BlockSpec((1, seq, POOL_WIDTH), lambda b: (b, 0, 0)),
                  pl.BlockSpec((ng, POOL_GROUP_DIM, POOL_GROUP_DIM), lambda b: (0, 0, 0)),
                  pl.BlockSpec((ng, 1, POOL_GROUP_DIM), lambda b: (0, 0, 0)),
                  pl.BlockSpec((ng, 1, POOL_GROUP_DIM), lambda b: (0, 0, 0))],
        out_specs=pl.BlockSpec((1, seq, POOL_WIDTH), lambda b: (b, 0, 0)),
        out_shape=jax.ShapeDtypeStruct((bsz, seq, POOL_WIDTH), BF16),
        scratch_shapes=[pltpu.VMEM((seq + 2 * POOL_PAD, POOL_WIDTH), F32)],
        compiler_params=_cparams(("parallel",)),
        name="pool_mix",
    )(z_lat, pool_w, pool_b.reshape(ng, 1, POOL_GROUP_DIM),
      pool_scale.reshape(ng, 1, POOL_GROUP_DIM))


def _outproj_kernel(x_ref, p_ref, a_ref, w_ref, g_ref, o_ref):
    r = (jnp.dot(p_ref[0], w_ref[pl.ds(0, POOL_WIDTH), :], preferred_element_type=F32)
         + jnp.dot(a_ref[0], w_ref[pl.ds(POOL_WIDTH, a_ref.shape[2]), :],
                   preferred_element_type=F32))
    o_ref[0] = x_ref[0] + g_ref[0] * r


def _outproj_call(x, pool_y, attn, w_out, mod3, tm=256):
    bsz, seq, d = x.shape
    aw = attn.shape[2]
    return pl.pallas_call(
        _outproj_kernel,
        grid=(bsz, seq // tm),
        in_specs=[pl.BlockSpec((1, tm, d), lambda b, i: (b, i, 0)),
                  pl.BlockSpec((1, tm, POOL_WIDTH), lambda b, i: (b, i, 0)),
                  pl.BlockSpec((1, tm, aw), lambda b, i: (b, i, 0)),
                  pl.BlockSpec(w_out.shape, lambda b, i: (0, 0)),
                  pl.BlockSpec((1, 1, d), lambda b, i: (b, 0, 2))],
        out_specs=pl.BlockSpec((1, tm, d), lambda b, i: (b, i, 0)),
        out_shape=jax.ShapeDtypeStruct((bsz, seq, d), F32),
        compiler_params=_cparams(("parallel", "parallel")),
        name="outproj_residual",
    )(x, pool_y, attn, w_out, mod3)


def _top16_rows(s):
    n, t = s.shape
    row = lax.broadcasted_iota(jnp.int32, (n, t), 0).astype(F32)
    row16 = lax.broadcasted_iota(jnp.int32, (PEER_TOPK, t), 0)
    work = s
    rank = jnp.full((n, t), 127.0, F32)
    vals = jnp.zeros((PEER_TOPK, t), F32)
    for i in range(PEER_TOPK):
        m = jnp.max(work, axis=0, keepdims=True)
        idx = jnp.min(jnp.where(work == m, row, float(n)), axis=0, keepdims=True)
        sel = row == idx
        rank = jnp.where(sel, float(i), rank)
        vals = jnp.where(row16 == i, m, vals)
        work = jnp.where(sel, NEG_INF, work)
    return vals, rank


_CAND_SLABS = ((0, 0, 8), (0, 8, 8), (1, 0, 8), (2, 0, 5), (3, 0, 4), (4, 0, 3),
               (5, 0, 2), (6, 0, 2), (7, 0, 2))


def _pair_select(va, vb):
    t = va.shape[1]
    sub = lax.broadcasted_iota(jnp.int32, (8, t), 0)
    subf = sub.astype(F32)
    cands, flats = [], []
    for (i, j0, lim) in _CAND_SLABS:
        c = va[i:i + 1, :] + vb[j0:j0 + 8, :]
        cands.append(jnp.where(sub < lim, c, NEG_INF))
        flats.append(subf + float(i * PEER_TOPK + j0))
    cands.append(va[8:16, :] + vb[0:1, :])
    flats.append(subf * float(PEER_TOPK) + float(8 * PEER_TOPK))
    cand = jnp.concatenate(cands, axis=0)
    flat = jnp.concatenate(flats, axis=0)
    work = cand
    big = float(PEER_TOPK * PEER_TOPK)
    for _ in range(PEER_TOPK):
        m = jnp.max(work, axis=0, keepdims=True)
        idx = jnp.min(jnp.where(work == m, flat, big), axis=0, keepdims=True)
        work = jnp.where(flat == idx, NEG_INF, work)
    sel = jnp.logical_and(work == NEG_INF, cand > NEG_INF)
    self32 = sel.astype(F32)
    top = va[0:1, :] + vb[0:1, :]
    z = jnp.sum(jnp.where(sel, jnp.exp(cand - top), 0.0), axis=0, keepdims=True)
    counts = [jnp.sum(self32[0:16], axis=0, keepdims=True)]
    for k in range(2, 9):
        counts.append(jnp.sum(self32[8 * k:8 * k + 8], axis=0, keepdims=True))
    n = jnp.concatenate(counts + [self32[72:80]], axis=0)
    return n, z


def _sort_network(n):
    size = 16
    pairs = []
    p = 1
    while p < size:
        k = p
        while k >= 1:
            for j in range(k % p, size - k, 2 * k):
                for i in range(min(k, size - j - k)):
                    if (i + j) // (2 * p) == (i + j + k) // (2 * p):
                        pairs.append((i + j, i + j + k))
            k //= 2
        p *= 2
    return [(i, j) for (i, j) in pairs if j < n]


def _sorted_top16(slabs):
    s = list(slabs)
    depth = len(s)
    for i, j in _sort_network(depth):
        s[i], s[j] = jnp.maximum(s[i], s[j]), jnp.minimum(s[i], s[j])
    t = s[0].shape[1]
    sub = lax.broadcasted_iota(jnp.int32, (8, t), 0).astype(F32)
    rows = []
    for i in range(PEER_TOPK):
        m = jnp.max(s[0], axis=0, keepdims=True)
        rows.append(m)
        if i == PEER_TOPK - 1:
            break
        first = jnp.min(jnp.where(s[0] == m, sub, 8.0), axis=0, keepdims=True)
        win = sub == first
        for r in range(min(depth, PEER_TOPK - 1 - i)):
            s[r] = jnp.where(win, s[r + 1] if r + 1 < depth else NEG_INF, s[r])
    return rows


def _tie_rows(s, v):
    cnt = jnp.sum(jnp.where(s >= v[PEER_TOPK - 1], 1.0, 0.0), axis=0, keepdims=True)
    tie = jnp.where(cnt != float(PEER_TOPK), 1.0, 0.0)
    for i in range(PEER_TOPK - 1):
        tie = tie + jnp.where(v[i] == v[i + 1], 1.0, 0.0)
    return tie


def _pair_select_fast(va, vb):
    t = va[0].shape[1]
    row16 = lax.broadcasted_iota(jnp.int32, (PEER_TOPK, t), 0)
    sub = lax.broadcasted_iota(jnp.int32, (8, t), 0)
    va_arr = jnp.zeros((PEER_TOPK, t), F32)
    vb_arr = jnp.zeros((PEER_TOPK, t), F32)
    for i in range(PEER_TOPK):
        va_arr = jnp.where(row16 == i, va[i], va_arr)
        vb_arr = jnp.where(row16 == i, vb[i], vb_arr)
    cands = []
    for (i, j0, lim) in _CAND_SLABS:
        cands.append(jnp.where(sub < lim, va[i] + vb_arr[j0:j0 + 8, :], NEG_INF))
    cands.append(va_arr[8:16, :] + vb[0])
    tau = _sorted_top16(cands)[PEER_TOPK - 1]
    top = va[0] + vb[0]
    sel = [jnp.where(c >= tau, 1.0, 0.0) for c in cands]
    z = sum(jnp.sum(s * jnp.exp(c - top), axis=0, keepdims=True) for s, c in zip(sel, cands))
    n = [jnp.sum(sel[0] + sel[1], axis=0, keepdims=True)]
    for k in range(2, 9):
        n.append(jnp.sum(sel[k], axis=0, keepdims=True))
    for r in range(8):
        n.append(sel[9][r:r + 1, :])
    total = sum(n[:8]) + jnp.sum(sel[9], axis=0, keepdims=True)
    return n, z, jnp.where(total != float(PEER_TOPK), 1.0, 0.0)


def _peer_sel_kernel(x_ref, g_ref, sh_ref, sc_ref, wq_ref, keys_ref,
                     f_ref, na_ref, ea_ref, rb_ref, eb_ref, qt_ref):
    f = _norm_mod(x_ref[0], g_ref[...], sh_ref[0], sc_ref[0]).astype(BF16)
    f_ref[0] = f
    nt = (((1,), (1,)), ((), ()))
    qt_ref[...] = lax.dot_general(wq_ref[...], f, nt, preferred_element_type=F32)

    def scores(hp):
        k_hi, k_lo = _split_bf16(keys_ref[hp])
        q_hi, q_lo = _split_bf16(qt_ref[pl.ds(pl.multiple_of(hp * PEER_HALF, PEER_HALF),
                                              PEER_HALF), :])
        return (jnp.dot(k_hi, q_hi, preferred_element_type=F32)
                + jnp.dot(k_hi, q_lo, preferred_element_type=F32)
                + jnp.dot(k_lo, q_hi, preferred_element_type=F32))

    def head(h, carry):
        sa = scores(2 * h)
        sb = scores(2 * h + 1)

        va = _sorted_top16([sa[8 * k:8 * k + 8] for k in range(PEER_NKEYS // 8)])
        vb = _sorted_top16([sb[8 * k:8 * k + 8] for k in range(PEER_NKEYS // 8)])
        n_rows, z, tie2 = _pair_select_fast(va, vb)
        na = jnp.zeros_like(sa)
        rank_b = jnp.zeros_like(sb)
        for i in range(PEER_TOPK):
            na = jnp.where(sa == va[i], n_rows[i], na)
            rank_b = rank_b + jnp.where(vb[i] > sb, 1.0, 0.0)
        na_ref[h] = na
        ea_ref[h] = jnp.exp(sa - va[0]) / z
        rb_ref[h] = rank_b.astype(BF16)
        eb_ref[h] = jnp.exp(sb - vb[0]).astype(BF16)

        tie = tie2 + _tie_rows(sa, va) + _tie_rows(sb, vb)

        @pl.when(jnp.max(tie) > 0.0)
        def _():
            va_t, rank_a_t = _top16_rows(sa)
            vb_t, rank_b_t = _top16_rows(sb)
            n_t, z_t = _pair_select(va_t, vb_t)
            na_t = jnp.zeros_like(sa)
            for i in range(PEER_TOPK):
                na_t = jnp.where(rank_a_t == float(i), n_t[i:i + 1, :], na_t)
            na_ref[h] = na_t
            ea_ref[h] = jnp.exp(sa - va_t[0:1, :]) / z_t
            rb_ref[h] = rank_b_t.astype(BF16)
            eb_ref[h] = jnp.exp(sb - vb_t[0:1, :]).astype(BF16)

        return carry

    lax.fori_loop(0, PEER_HEADS, head, 0)


def _peer_sel_call(x1, norm_g, mod3, wq_t, keys, ts=512):
    bsz, seq, d = x1.shape
    tokens = bsz * seq
    nblk = seq // ts
    sel_f32 = jax.ShapeDtypeStruct((PEER_HEADS, PEER_NKEYS, tokens), F32)
    sel_bf16 = jax.ShapeDtypeStruct((PEER_HEADS, PEER_NKEYS, tokens), BF16)
    sel_spec = pl.BlockSpec((PEER_HEADS, PEER_NKEYS, ts), lambda b, i: (0, 0, b * nblk + i))
    return pl.pallas_call(
        _peer_sel_kernel,
        grid=(bsz, nblk),
        in_specs=[pl.BlockSpec((1, ts, d), lambda b, i: (b, i, 0)),
                  pl.BlockSpec((1, d), lambda b, i: (0, 0)),
                  pl.BlockSpec((1, 1, d), lambda b, i: (b, 0, 3)),
                  pl.BlockSpec((1, 1, d), lambda b, i: (b, 0, 4)),
                  pl.BlockSpec(wq_t.shape, lambda b, i: (0, 0)),
                  pl.BlockSpec(keys.shape, lambda b, i: (0, 0, 0))],
        out_specs=[pl.BlockSpec((1, ts, d), lambda b, i: (b, i, 0)),
                   sel_spec, sel_spec, sel_spec, sel_spec],
        out_shape=[jax.ShapeDtypeStruct((bsz, seq, d), BF16),
                   sel_f32, sel_f32, sel_bf16, sel_bf16],
        scratch_shapes=[pltpu.VMEM((wq_t.shape[0], ts), F32)],
        compiler_params=_cparams(("parallel", "parallel")),
        name="peer_select",
    )(x1, norm_g, mod3, mod3, wq_t, keys)


def _gelu(x):
    return 0.5 * x * (1.0 + lax.erf(x * (1.0 / math.sqrt(2.0))))


def _peer_kernel(f_ref, u_ref, vt_ref, na_ref, ea_ref, rb_ref, eb_ref, x_ref, g2_ref, fg_ref,
                 o_ref, act_ref, p_ref, acc_ref, *, a_per_step):
    e = pl.program_id(2)
    tm = f_ref.shape[1]

    @pl.when(e == 0)
    def _():
        acc_ref[...] = jnp.zeros_like(acc_ref)

    nt = (((1,), (1,)), ((), ()))
    assert sum(PEER_SUBCHUNK_KEYS) == a_per_step
    first_key = [sum(PEER_SUBCHUNK_KEYS[:j]) for j in range(len(PEER_SUBCHUNK_KEYS))]
    zero = jnp.zeros((PEER_NKEYS, BF16_TILE_LANES), BF16)

    def rows_of(j):
        return pl.ds(first_key[j] * PEER_NKEYS, PEER_SUBCHUNK_KEYS[j] * PEER_NKEYS)

    def local_rows(j):
        return pl.ds(0, PEER_SUBCHUNK_KEYS[j] * PEER_NKEYS)

    def activations(j):
        act_ref[j % 2, local_rows(j), :] = lax.dot_general(
            u_ref[rows_of(j), :], f_ref[0], nt, preferred_element_type=F32)

    def gated(j):
        for lc in range(tm // BF16_TILE_LANES):
            cols = pl.ds(lc * BF16_TILE_LANES, BF16_TILE_LANES)
            for al in range(PEER_SUBCHUNK_KEYS[j]):
                a = first_key[j] + al
                w = zero
                for h in range(PEER_HEADS):
                    na = na_ref[h, pl.ds(a, 1), cols].astype(BF16)
                    ea = ea_ref[h, pl.ds(a, 1), cols].astype(BF16)
                    w = w + jnp.where(rb_ref[h, :, cols] < na, eb_ref[h, :, cols], zero) * ea
                rows = pl.ds(al * PEER_NKEYS, PEER_NKEYS)
                p_ref[j % 2, rows, cols] = _gelu(act_ref[j % 2, rows, cols]).astype(BF16) * w

    def values(j):
        acc_ref[...] += jnp.dot(vt_ref[:, rows_of(j)], p_ref[j % 2, local_rows(j), :],
                                preferred_element_type=F32)

    n_sub = len(PEER_SUBCHUNK_KEYS)
    for stage in range(n_sub + 2):
        if stage < n_sub:
            activations(stage)
        if 1 <= stage <= n_sub:
            gated(stage - 1)
        if stage >= 2:
            values(stage - 2)

    @pl.when(e == pl.num_programs(2) - 1)
    def _():
        y = x_ref[0] + g2_ref[0] * acc_ref[...].T
        ms = jnp.mean(y * y, axis=1, keepdims=True)
        o_ref[0] = y * lax.rsqrt(ms + EPS) * fg_ref[...]


def _peer_call(f, u, v_t, na, ea, rb, eb, x1, mod3, final_g, tm=512, ec=1024):
    bsz, seq, d = x1.shape
    experts = u.shape[0]
    nblk = seq // tm
    a_per_step = ec // PEER_NKEYS
    kern = functools.partial(_peer_kernel, a_per_step=a_per_step)
    tok = lambda b, i, e: b * nblk + i
    return pl.pallas_call(
        kern,
        grid=(bsz, nblk, experts // ec),
        in_specs=[pl.BlockSpec((1, tm, d), lambda b, i, e: (b, i, 0)),
                  pl.BlockSpec((ec, d), lambda b, i, e: (e, 0)),
                  pl.BlockSpec((d, ec), lambda b, i, e: (0, e)),
                  pl.BlockSpec((PEER_HEADS, a_per_step, tm), lambda b, i, e: (0, e, tok(b, i, e))),
                  pl.BlockSpec((PEER_HEADS, a_per_step, tm), lambda b, i, e: (0, e, tok(b, i, e))),
                  pl.BlockSpec((PEER_HEADS, PEER_NKEYS, tm), lambda b, i, e: (0, 0, tok(b, i, e))),
                  pl.BlockSpec((PEER_HEADS, PEER_NKEYS, tm), lambda b, i, e: (0, 0, tok(b, i, e))),
                  pl.BlockSpec((1, tm, d), lambda b, i, e: (b, i, 0)),
                  pl.BlockSpec((1, 1, d), lambda b, i, e: (b, 0, 5)),
                  pl.BlockSpec((1, d), lambda b, i, e: (0, 0))],
        out_specs=pl.BlockSpec((1, tm, d), lambda b, i, e: (b, i, 0)),
        out_shape=jax.ShapeDtypeStruct((bsz, seq, d), F32),
        scratch_shapes=[pltpu.VMEM((2, max(PEER_SUBCHUNK_KEYS) * PEER_NKEYS, tm), F32),
                        pltpu.VMEM((2, max(PEER_SUBCHUNK_KEYS) * PEER_NKEYS, tm), BF16),
                        pltpu.VMEM((d, tm), F32)],
        compiler_params=_cparams(("parallel", "parallel", "arbitrary")),
        name="peer_dense",
    )(f, u, v_t, na, ea, rb, eb, x1, mod3, final_g)


def _rope_tables(seq):
    pos = np.arange(seq)
    row = (pos // GRID_W).astype(np.float32)
    col = (pos % GRID_W).astype(np.float32)
    inv_freq = (ROPE_BASE ** (-np.arange(ROPE_HALF, dtype=np.float32) / ROPE_HALF)).astype(np.float32)
    lane = np.arange(LANES)
    axis = (lane % DIFF_QKDIM) // ROPE_AXIS_DIM
    freq = inv_freq[lane % ROPE_HALF]
    p = np.where(axis[None, :] == 0, row[:, None], col[:, None]).astype(np.float32)
    ang = p * freq[None, :]
    sign = np.where((lane % ROPE_AXIS_DIM) < ROPE_HALF, -1.0, 1.0).astype(np.float32)
    return jnp.asarray(np.cos(ang), F32), jnp.asarray(np.sin(ang) * sign[None, :], F32)


def kernel(x, c, ctx, c_ctx, ada_w, ada_b, norm1_g, w_in, pool_w, pool_b, pool_scale,
           diff_lambda, subln_g, w_out, norm2_g, peer_wq, peer_keys, peer_u, peer_v, final_g):
    bsz, seq, d = x.shape
    ctx_len = ctx.shape[1]
    layer = 0
    lam_init = 0.8 - 0.6 * math.exp(-0.3 * layer)
    mod_rows = 8
    assert bsz + 1 <= mod_rows

    cc = jnp.concatenate([c, c_ctx[None, :], jnp.zeros((mod_rows - bsz - 1, d), F32)], axis=0)
    mod = _ada_call(cc, ada_w[layer], ada_b[layer][None, :])
    mod3 = mod.reshape(mod_rows, 1, 6 * d)

    w_in_b = w_in[layer].astype(BF16)
    g1 = norm1_g[layer][None, :]
    cos, sin = _rope_tables(seq)
    in_width = w_in_b.shape[1]
    attn_width = (in_width - POOL_WIDTH) // 3
    z_lat = _inproj_call(x, g1, mod3, lambda b: b, w_in_b, cos, sin,
                         rope=True, col0=0, ncols=in_width)
    kv_ctx = _inproj_call(ctx, g1, mod3, lambda b: bsz, w_in_b, cos, sin,
                          rope=False, col0=POOL_WIDTH + attn_width, ncols=2 * attn_width)

    attn, u_b, v_t = _attn_call(z_lat, kv_ctx, diff_lambda[layer], subln_g[layer][None, :],
                                peer_u[layer], peer_v[layer], lam_init=lam_init)
    pool_y = _pool_call(z_lat, pool_w[layer], pool_b[layer], pool_scale[layer])
    x1 = _outproj_call(x, pool_y, attn, w_out[layer].astype(BF16), mod3)

    wq_t = peer_wq[layer].T.astype(BF16)
    keys = peer_keys[layer].reshape(2 * PEER_HEADS, PEER_NKEYS, PEER_HALF)
    f, na, ea, rb, eb = _peer_sel_call(x1, norm2_g[layer][None, :], mod3, wq_t, keys)
    return _peer_call(f, u_b, v_t, na, ea, rb, eb, x1, mod3, final_g[None, :])
```

```python
import functools
import math

import jax
import jax.numpy as jnp
import numpy as np
from jax import lax
from jax.experimental import pallas as pl
from jax.experimental.pallas import tpu as pltpu

F32 = jnp.float32
BF16 = jnp.bfloat16

EPS = 1e-6
GRID_W = 64
POOL_WINDOWS = (2, 4, 8, 16)
POOL_GROUP_DIM = 128
POOL_WIDTH = POOL_GROUP_DIM * len(POOL_WINDOWS)
DIFF_VDIM = 128
DIFF_QKDIM = 64
ROPE_BASE = 10000.0
ROPE_AXIS_DIM = 32
ROPE_HALF = ROPE_AXIS_DIM // 2
PEER_HEADS = 8
PEER_NKEYS = 128
PEER_HALF = 128
PEER_TOPK = 16
LANES = 128
BF16_TILE_LANES = 256
SOFTMAX_ROWS = 16
PEER_SUBCHUNK_KEYS = (4, 4)
POOL_PAD = 8
VMEM_LIMIT_BYTES = 56 * 1024 * 1024
NEG_INF = float("-inf")


def _cparams(sem):
    return pltpu.CompilerParams(dimension_semantics=sem, vmem_limit_bytes=VMEM_LIMIT_BYTES)


def _split_bf16(a):
    hi = a.astype(BF16)
    lo = (a - hi.astype(F32)).astype(BF16)
    return hi, lo


def _ada_kernel(c_ref, w_ref, b_ref, o_ref):
    c = c_ref[...]
    s = c * (1.0 / (1.0 + jnp.exp(-c)))
    o_ref[...] = jnp.dot(s.astype(BF16), w_ref[...].astype(BF16),
                         preferred_element_type=F32) + b_ref[...]


def _ada_call(cc, w, b, tn=1024):
    rows, d = cc.shape
    n = w.shape[1]
    return pl.pallas_call(
        _ada_kernel,
        grid=(n // tn,),
        in_specs=[pl.BlockSpec((rows, d), lambda j: (0, 0)),
                  pl.BlockSpec((d, tn), lambda j: (0, j)),
                  pl.BlockSpec((1, tn), lambda j: (0, j))],
        out_specs=pl.BlockSpec((rows, tn), lambda j: (0, j)),
        out_shape=jax.ShapeDtypeStruct((rows, n), F32),
        compiler_params=_cparams(("arbitrary",)),
        name="ada_mod",
    )(cc, w, b)


def _norm_mod(xf, g, sh, sc):
    ms = jnp.mean(xf * xf, axis=-1, keepdims=True)
    y = xf * lax.rsqrt(ms + EPS) * g
    return y * (1.0 + sc) + sh


def _inproj_kernel(x_ref, g_ref, sh_ref, sc_ref, w_ref, cos_ref, sin_ref, o_ref, h_ref, z_ref,
                   *, rope, col0, tn):
    n_tiles = o_ref.shape[2] // tn
    q_tiles = (DIFF_VDIM * 12) // tn
    pool_tiles = POOL_WIDTH // tn

    def matmul(n):
        z_ref[n % 2] = jnp.dot(h_ref[...], w_ref[:, pl.ds(col0 + n * tn, tn)],
                               preferred_element_type=F32)

    def epilogue(n):
        z = z_ref[n % 2]
        if rope and pool_tiles <= n < pool_tiles + 2 * q_tiles:
            reps = tn // LANES
            cos = jnp.concatenate([cos_ref[...]] * reps, axis=1)
            sin = jnp.concatenate([sin_ref[...]] * reps, axis=1)
            lane = lax.broadcasted_iota(jnp.int32, z.shape, 1)
            first = (lane % ROPE_AXIS_DIM) < ROPE_HALF
            partner = jnp.where(first, pltpu.roll(z, tn - ROPE_HALF, 1),
                                pltpu.roll(z, ROPE_HALF, 1))
            z = z * cos + partner * sin
            if n < pool_tiles + q_tiles:
                z = z * DIFF_QKDIM ** -0.5
        o_ref[0, :, pl.ds(n * tn, tn)] = z.astype(BF16)

    h = _norm_mod(x_ref[0], g_ref[...], sh_ref[0], sc_ref[0])
    h_ref[...] = h.astype(BF16)
    for stage in range(n_tiles + 1):
        if stage < n_tiles:
            matmul(stage)
        if stage >= 1:
            epilogue(stage - 1)


def _inproj_call(x, g, mod3, mod_row_fn, w, cos, sin, *, rope, col0, ncols, tm=512, tn=512):
    bsz, rows, d = x.shape
    tm = min(tm, rows)
    kern = functools.partial(_inproj_kernel, rope=rope, col0=col0, tn=tn)
    return pl.pallas_call(
        kern,
        grid=(bsz, rows // tm),
        in_specs=[pl.BlockSpec((1, tm, d), lambda b, i: (b, i, 0)),
                  pl.BlockSpec((1, d), lambda b, i: (0, 0)),
                  pl.BlockSpec((1, 1, d), lambda b, i: (mod_row_fn(b), 0, 0)),
                  pl.BlockSpec((1, 1, d), lambda b, i: (mod_row_fn(b), 0, 1)),
                  pl.BlockSpec(w.shape, lambda b, i: (0, 0), pipeline_mode=pl.Buffered(1)),
                  pl.BlockSpec((tm, LANES), lambda b, i: (i, 0)),
                  pl.BlockSpec((tm, LANES), lambda b, i: (i, 0))],
        out_specs=pl.BlockSpec((1, tm, ncols), lambda b, i: (b, i, 0)),
        out_shape=jax.ShapeDtypeStruct((bsz, rows, ncols), BF16),
        scratch_shapes=[pltpu.VMEM((tm, d), BF16),
                        pltpu.VMEM((2, tm, tn), F32)],
        compiler_params=_cparams(("parallel", "arbitrary")),
        name="inproj_rope" if rope else "inproj_ctx",
    )(x, g, mod3, mod3, w, cos, sin)


def _attn_kernel(q_ref, kc_ref, vc_ref, kl_ref, vl_ref, dl_ref, sg_ref, u_ref, v_ref, wq_ref,
                 wo_ref, o_ref, ub_ref, vt_ref, wqt_ref, wob_ref, s_ref, e_ref, va_ref,
                 *, lam_init, sub, side_heads):
    ctx_len = kc_ref.shape[1]
    seq = kl_ref.shape[1]
    n_sub = q_ref.shape[1] // sub
    lq = dl_ref[...]
    lam = (jnp.exp(jnp.sum(lq[0:1] * lq[1:2], axis=1, keepdims=True))
           - jnp.exp(jnp.sum(lq[2:3] * lq[3:4], axis=1, keepdims=True)) + lam_init)
    nt = (((1,), (1,)), ((), ()))
    lane = lax.broadcasted_iota(jnp.int32, (sub, LANES), 1)
    zero = jnp.zeros((sub, LANES), BF16)

    ones_col = jnp.where(lax.broadcasted_iota(jnp.int32, (ctx_len + seq, LANES), 1) == 0,
                         1.0, 0.0).astype(BF16)
    va_ref[pl.ds(0, ctx_len), pl.ds(0, LANES)] = vc_ref[0]
    va_ref[pl.ds(ctx_len, seq), pl.ds(0, LANES)] = vl_ref[0]
    va_ref[:, pl.ds(LANES, LANES)] = ones_col

    def scores(t):
        q = q_ref[0, pl.ds(t * sub, sub), :]
        q2 = jnp.concatenate([jnp.where(lane < DIFF_QKDIM, q, zero),
                              jnp.where(lane >= DIFF_QKDIM, q, zero)], axis=0)
        s_ref[t % 2, :, pl.ds(0, ctx_len)] = lax.dot_general(
            q2, kc_ref[0], nt, preferred_element_type=F32)
        s_ref[t % 2, :, pl.ds(ctx_len, seq)] = lax.dot_general(
            q2, kl_ref[0], nt, preferred_element_type=F32)

    def numerators(t):
        for r in range(0, 2 * sub, SOFTMAX_ROWS):
            rows = pl.ds(r, SOFTMAX_ROWS)
            s = s_ref[t % 2, rows, :]
            e_ref[t % 2, rows, :] = jnp.exp(s - jnp.max(s, axis=1, keepdims=True)).astype(BF16)

    def outputs(t):
        oa = jnp.dot(e_ref[t % 2], va_ref[...], preferred_element_type=F32)
        o2 = oa[:, :LANES] / oa[:, LANES:LANES + 1]
        o = o2[:sub] - lam * o2[sub:]
        ms = jnp.mean(o * o, axis=1, keepdims=True)
        y = o * lax.rsqrt(ms + EPS) * sg_ref[...]
        o_ref[0, pl.ds(t * sub, sub), :] = (y * (1.0 - lam_init)).astype(BF16)

    for stage in range(n_sub + 2):
        if stage < n_sub:
            scores(stage)
        if 1 <= stage <= n_sub:
            numerators(stage - 1)
        if stage >= 2:
            outputs(stage - 2)

    @pl.when(jnp.logical_and(pl.program_id(1) < side_heads, pl.program_id(2) == 0))
    def _():
        ub_ref[...] = u_ref[...].astype(BF16)
        vt_ref[...] = v_ref[...].T.astype(BF16)

    @pl.when(jnp.logical_and(pl.program_id(1) >= side_heads, pl.program_id(2) == 0))
    def _():
        wqt_ref[...] = wq_ref[...].T.astype(BF16)
        wob_ref[...] = wo_ref[...].astype(BF16)


def _attn_call(z_lat, kv_ctx, diff_lambda, subln_g, peer_u, peer_v, peer_wq, w_out, *, lam_init,
               tq=2048, sub=256):
    bsz, seq, _ = z_lat.shape
    ctx_len = kv_ctx.shape[1]
    tq = min(tq, seq)
    heads = 12
    qb = POOL_WIDTH // LANES
    kb = qb + heads
    vb = kb + heads
    experts, d = peer_u.shape
    side_heads = 8
    trows = experts // (bsz * side_heads)
    assert trows * bsz * side_heads == experts
    tblk = lambda b, h, i: b * side_heads + jnp.minimum(h, side_heads - 1)
    rest = heads - side_heads
    wrows = peer_wq.shape[0] // (bsz * rest)
    assert wrows * bsz * rest == peer_wq.shape[0] == w_out.shape[0] and wrows % LANES == 0
    wblk = lambda b, h, i: b * rest + jnp.maximum(h - side_heads, 0)
    kern = functools.partial(_attn_kernel, lam_init=lam_init, sub=sub, side_heads=side_heads)
    return pl.pallas_call(
        kern,
        grid=(bsz, heads, seq // tq),
        in_specs=[pl.BlockSpec((1, tq, LANES), lambda b, h, i: (b, i, qb + h)),
                  pl.BlockSpec((1, ctx_len, LANES), lambda b, h, i: (b, 0, h)),
                  pl.BlockSpec((1, ctx_len, LANES), lambda b, h, i: (b, 0, heads + h)),
                  pl.BlockSpec((1, seq, LANES), lambda b, h, i: (b, 0, kb + h)),
                  pl.BlockSpec((1, seq, LANES), lambda b, h, i: (b, 0, vb + h)),
                  pl.BlockSpec(diff_lambda.shape, lambda b, h, i: (0, 0)),
                  pl.BlockSpec((1, LANES), lambda b, h, i: (0, 0)),
                  pl.BlockSpec((trows, d), lambda b, h, i: (tblk(b, h, i), 0)),
                  pl.BlockSpec((trows, d), lambda b, h, i: (tblk(b, h, i), 0)),
                  pl.BlockSpec((wrows, peer_wq.shape[1]), lambda b, h, i: (wblk(b, h, i), 0)),
                  pl.BlockSpec((wrows, w_out.shape[1]), lambda b, h, i: (wblk(b, h, i), 0))],
        out_specs=[pl.BlockSpec((1, tq, LANES), lambda b, h, i: (b, i, h)),
                   pl.BlockSpec((trows, d), lambda b, h, i: (tblk(b, h, i), 0)),
                   pl.BlockSpec((d, trows), lambda b, h, i: (0, tblk(b, h, i))),
                   pl.BlockSpec((peer_wq.shape[1], wrows), lambda b, h, i: (0, wblk(b, h, i))),
                   pl.BlockSpec((wrows, w_out.shape[1]), lambda b, h, i: (wblk(b, h, i), 0))],
        out_shape=[jax.ShapeDtypeStruct((bsz, seq, heads * DIFF_VDIM), BF16),
                   jax.ShapeDtypeStruct((experts, d), BF16),
                   jax.ShapeDtypeStruct((d, experts), BF16),
                   jax.ShapeDtypeStruct(peer_wq.shape[::-1], BF16),
                   jax.ShapeDtypeStruct(w_out.shape, BF16)],
        scratch_shapes=[pltpu.VMEM((2, 2 * sub, ctx_len + seq), F32),
                        pltpu.VMEM((2, 2 * sub, ctx_len + seq), BF16),
                        pltpu.VMEM((ctx_len + seq, 2 * LANES), BF16)],
        compiler_params=_cparams(("parallel", "arbitrary", "arbitrary")),
        name="diff_attn",
    )(z_lat, kv_ctx, kv_ctx, z_lat, z_lat, diff_lambda, subln_g, peer_u, peer_v, peer_wq, w_out)


def _pool_kernel(z_ref, w_ref, b_ref, s_ref, o_ref, zp_ref):
    seq = z_ref.shape[1]
    zp_ref[...] = jnp.zeros_like(zp_ref)
    zp_ref[pl.ds(POOL_PAD, seq), :] = z_ref[0].astype(F32)
    t = lax.broadcasted_iota(jnp.int32, (seq, POOL_GROUP_DIM), 0)
    outs = []
    for g, win in enumerate(POOL_WINDOWS):
        half = win // 2
        cols = pl.ds(g * POOL_GROUP_DIM, POOL_GROUP_DIM)
        acc = jnp.zeros((seq, POOL_GROUP_DIM), F32)
        for k in range(-half, half):
            acc = acc + zp_ref[pl.ds(POOL_PAD + k, seq), cols]
        cnt = (jnp.minimum(t + half, seq) - jnp.maximum(t - half, 0)).astype(F32)
        y = acc / cnt - zp_ref[pl.ds(POOL_PAD, seq), cols]
        r = jnp.dot(y.astype(BF16), w_ref[g].astype(BF16), preferred_element_type=F32)
        outs.append((r + b_ref[g]) * s_ref[g])
    o_ref[0] = jnp.concatenate(outs, axis=1).astype(BF16)


def _pool_call(z_lat, pool_w, pool_b, pool_scale):
    bsz, seq, _ = z_lat.shape
    ng = len(POOL_WINDOWS)
    return pl.pallas_call(
        _pool_kernel,
        grid=(bsz,),
        in_specs=[pl.BlockSpec((1, seq, POOL_WIDTH), lambda b: (b, 0, 0)),
                  pl.BlockSpec((ng, POOL_GROUP_DIM, POOL_GROUP_DIM), lambda b: (0, 0, 0)),
                  pl.BlockSpec((ng, 1, POOL_GROUP_DIM), lambda b: (0, 0, 0)),
                  pl.BlockSpec((ng, 1, POOL_GROUP_DIM), lambda b: (0, 0, 0))],
        out_specs=pl.BlockSpec((1, seq, POOL_WIDTH), lambda b: (b, 0, 0)),
        out_shape=jax.ShapeDtypeStruct((bsz, seq, POOL_WIDTH), BF16),
        scratch_shapes=[pltpu.VMEM((seq + 2 * POOL_PAD, POOL_WIDTH), F32)],
        compiler_params=_cparams(("parallel",)),
        name="pool_mix",
    )(z_lat, pool_w, pool_b.reshape(ng, 1, POOL_GROUP_DIM),
      pool_scale.reshape(ng, 1, POOL_GROUP_DIM))


def _outproj_kernel(x_ref, p_ref, a_ref, w_ref, g_ref, o_ref):
    r = (jnp.dot(p_ref[0], w_ref[pl.ds(0, POOL_WIDTH), :], preferred_element_type=F32)
         + jnp.dot(a_ref[0], w_ref[pl.ds(POOL_WIDTH, a_ref.shape[2]), :],
                   preferred_element_type=F32))
    o_ref[0] = x_ref[0] + g_ref[0] * r


def _outproj_call(x, pool_y, attn, w_out, mod3, tm=256):
    bsz, seq, d = x.shape
    aw = attn.shape[2]
    return pl.pallas_call(
        _outproj_kernel,
        grid=(bsz, seq // tm),
        in_specs=[pl.BlockSpec((1, tm, d), lambda b, i: (b, i, 0)),
                  pl.BlockSpec((1, tm, POOL_WIDTH), lambda b, i: (b, i, 0)),
                  pl.BlockSpec((1, tm, aw), lambda b, i: (b, i, 0)),
                  pl.BlockSpec(w_out.shape, lambda b, i: (0, 0)),
                  pl.BlockSpec((1, 1, d), lambda b, i: (b, 0, 2))],
        out_specs=pl.BlockSpec((1, tm, d), lambda b, i: (b, i, 0)),
        out_shape=jax.ShapeDtypeStruct((bsz, seq, d), F32),
        compiler_params=_cparams(("parallel", "parallel")),
        name="outproj_residual",
    )(x, pool_y, attn, w_out, mod3)


def _top16_rows(s):
    n, t = s.shape
    row = lax.broadcasted_iota(jnp.int32, (n, t), 0).astype(F32)
    row16 = lax.broadcasted_iota(jnp.int32, (PEER_TOPK, t), 0)
    work = s
    rank = jnp.full((n, t), 127.0, F32)
    vals = jnp.zeros((PEER_TOPK, t), F32)
    for i in range(PEER_TOPK):
        m = jnp.max(work, axis=0, keepdims=True)
        idx = jnp.min(jnp.where(work == m, row, float(n)), axis=0, keepdims=True)
        sel = row == idx
        rank = jnp.where(sel, float(i), rank)
        vals = jnp.where(row16 == i, m, vals)
        work = jnp.where(sel, NEG_INF, work)
    return vals, rank


_CAND_SLABS = ((0, 0, 8), (0, 8, 8), (1, 0, 8), (2, 0, 5), (3, 0, 4), (4, 0, 3),
               (5, 0, 2), (6, 0, 2), (7, 0, 2))


def _pair_select(va, vb):
    t = va.shape[1]
    sub = lax.broadcasted_iota(jnp.int32, (8, t), 0)
    subf = sub.astype(F32)
    cands, flats = [], []
    for (i, j0, lim) in _CAND_SLABS:
        c = va[i:i + 1, :] + vb[j0:j0 + 8, :]
        cands.append(jnp.where(sub < lim, c, NEG_INF))
        flats.append(subf + float(i * PEER_TOPK + j0))
    cands.append(va[8:16, :] + vb[0:1, :])
    flats.append(subf * float(PEER_TOPK) + float(8 * PEER_TOPK))
    cand = jnp.concatenate(cands, axis=0)
    flat = jnp.concatenate(flats, axis=0)
    work = cand
    big = float(PEER_TOPK * PEER_TOPK)
    for _ in range(PEER_TOPK):
        m = jnp.max(work, axis=0, keepdims=True)
        idx = jnp.min(jnp.where(work == m, flat, big), axis=0, keepdims=True)
        work = jnp.where(flat == idx, NEG_INF, work)
    sel = jnp.logical_and(work == NEG_INF, cand > NEG_INF)
    self32 = sel.astype(F32)
    top = va[0:1, :] + vb[0:1, :]
    z = jnp.sum(jnp.where(sel, jnp.exp(cand - top), 0.0), axis=0, keepdims=True)
    counts = [jnp.sum(self32[0:16], axis=0, keepdims=True)]
    for k in range(2, 9):
        counts.append(jnp.sum(self32[8 * k:8 * k + 8], axis=0, keepdims=True))
    n = jnp.concatenate(counts + [self32[72:80]], axis=0)
    return n, z


def _sort_network(n):
    size = 16
    pairs = []
    p = 1
    while p < size:
        k = p
        while k >= 1:
            for j in range(k % p, size - k, 2 * k):
                for i in range(min(k, size - j - k)):
                    if (i + j) // (2 * p) == (i + j + k) // (2 * p):
                        pairs.append((i + j, i + j + k))
            k //= 2
        p *= 2
    return [(i, j) for (i, j) in pairs if j < n]


def _sorted_top16(slabs):
    s = list(slabs)
    depth = len(s)
    for i, j in _sort_network(depth):
        s[i], s[j] = jnp.maximum(s[i], s[j]), jnp.minimum(s[i], s[j])
    t = s[0].shape[1]
    sub = lax.broadcasted_iota(jnp.int32, (8, t), 0).astype(F32)
    rows = []
    for i in range(PEER_TOPK):
        m = jnp.max(s[0], axis=0, keepdims=True)
        rows.append(m)
        if i == PEER_TOPK - 1:
            break
        first = jnp.min(jnp.where(s[0] == m, sub, 8.0), axis=0, keepdims=True)
        win = sub == first
        for r in range(min(depth, PEER_TOPK - 1 - i)):
            s[r] = jnp.where(win, s[r + 1] if r + 1 < depth else NEG_INF, s[r])
    return rows


def _tie_rows(s, v):
    cnt = jnp.sum(jnp.where(s >= v[PEER_TOPK - 1], 1.0, 0.0), axis=0, keepdims=True)
    tie = jnp.where(cnt != float(PEER_TOPK), 1.0, 0.0)
    for i in range(PEER_TOPK - 1):
        tie = tie + jnp.where(v[i] == v[i + 1], 1.0, 0.0)
    return tie


def _pair_select_fast(va, vb):
    t = va[0].shape[1]
    row16 = lax.broadcasted_iota(jnp.int32, (PEER_TOPK, t), 0)
    sub = lax.broadcasted_iota(jnp.int32, (8, t), 0)
    va_arr = jnp.zeros((PEER_TOPK, t), F32)
    vb_arr = jnp.zeros((PEER_TOPK, t), F32)
    for i in range(PEER_TOPK):
        va_arr = jnp.where(row16 == i, va[i], va_arr)
        vb_arr = jnp.where(row16 == i, vb[i], vb_arr)
    cands = []
    for (i, j0, lim) in _CAND_SLABS:
        cands.append(jnp.where(sub < lim, va[i] + vb_arr[j0:j0 + 8, :], NEG_INF))
    cands.append(va_arr[8:16, :] + vb[0])
    tau = _sorted_top16(cands)[PEER_TOPK - 1]
    top = va[0] + vb[0]
    sel = [jnp.where(c >= tau, 1.0, 0.0) for c in cands]
    z = sum(jnp.sum(s * jnp.exp(c - top), axis=0, keepdims=True) for s, c in zip(sel, cands))
    n = [jnp.sum(sel[0] + sel[1], axis=0, keepdims=True)]
    for k in range(2, 9):
        n.append(jnp.sum(sel[k], axis=0, keepdims=True))
    for r in range(8):
        n.append(sel[9][r:r + 1, :])
    total = sum(n[:8]) + jnp.sum(sel[9], axis=0, keepdims=True)
    return n, z, jnp.where(total != float(PEER_TOPK), 1.0, 0.0)


def _peer_sel_kernel(x_ref, g_ref, sh_ref, sc_ref, wq_ref, keys_ref,
                     f_ref, na_ref, ea_ref, rb_ref, eb_ref, qt_ref):
    f = _norm_mod(x_ref[0], g_ref[...], sh_ref[0], sc_ref[0]).astype(BF16)
    f_ref[0] = f
    nt = (((1,), (1,)), ((), ()))
    qt_ref[...] = lax.dot_general(wq_ref[...], f, nt, preferred_element_type=F32)

    def scores(hp):
        k_hi, k_lo = _split_bf16(keys_ref[hp])
        q_hi, q_lo = _split_bf16(qt_ref[pl.ds(pl.multiple_of(hp * PEER_HALF, PEER_HALF),
                                              PEER_HALF), :])
        return (jnp.dot(k_hi, q_hi, preferred_element_type=F32)
                + jnp.dot(k_hi, q_lo, preferred_element_type=F32)
                + jnp.dot(k_lo, q_hi, preferred_element_type=F32))

    def head(h, carry):
        sa = scores(2 * h)
        sb = scores(2 * h + 1)

        va = _sorted_top16([sa[8 * k:8 * k + 8] for k in range(PEER_NKEYS // 8)])
        vb = _sorted_top16([sb[8 * k:8 * k + 8] for k in range(PEER_NKEYS // 8)])
        n_rows, z, tie2 = _pair_select_fast(va, vb)
        na = jnp.zeros_like(sa)
        rank_b = jnp.zeros_like(sb)
        for i in range(PEER_TOPK):
            na = jnp.where(sa == va[i], n_rows[i], na)
            rank_b = rank_b + jnp.where(vb[i] > sb, 1.0, 0.0)
        na_ref[h] = na
        ea_ref[h] = jnp.exp(sa - va[0]) / z
        rb_ref[h] = rank_b.astype(BF16)
        eb_ref[h] = jnp.exp(sb - vb[0]).astype(BF16)

        tie = tie2 + _tie_rows(sa, va) + _tie_rows(sb, vb)

        @pl.when(jnp.max(tie) > 0.0)
        def _():
            va_t, rank_a_t = _top16_rows(sa)
            vb_t, rank_b_t = _top16_rows(sb)
            n_t, z_t = _pair_select(va_t, vb_t)
            na_t = jnp.zeros_like(sa)
            for i in range(PEER_TOPK):
                na_t = jnp.where(rank_a_t == float(i), n_t[i:i + 1, :], na_t)
            na_ref[h] = na_t
            ea_ref[h] = jnp.exp(sa - va_t[0:1, :]) / z_t
            rb_ref[h] = rank_b_t.astype(BF16)
            eb_ref[h] = jnp.exp(sb - vb_t[0:1, :]).astype(BF16)

        return carry

    lax.fori_loop(0, PEER_HEADS, head, 0)


def _peer_sel_call(x1, norm_g, mod3, wq_t, keys, ts=512):
    bsz, seq, d = x1.shape
    tokens = bsz * seq
    nblk = seq // ts
    sel_f32 = jax.ShapeDtypeStruct((PEER_HEADS, PEER_NKEYS, tokens), F32)
    sel_bf16 = jax.ShapeDtypeStruct((PEER_HEADS, PEER_NKEYS, tokens), BF16)
    sel_spec = pl.BlockSpec((PEER_HEADS, PEER_NKEYS, ts), lambda b, i: (0, 0, b * nblk + i))
    return pl.pallas_call(
        _peer_sel_kernel,
        grid=(bsz, nblk),
        in_specs=[pl.BlockSpec((1, ts, d), lambda b, i: (b, i, 0)),
                  pl.BlockSpec((1, d), lambda b, i: (0, 0)),
                  pl.BlockSpec((1, 1, d), lambda b, i: (b, 0, 3)),
                  pl.BlockSpec((1, 1, d), lambda b, i: (b, 0, 4)),
                  pl.BlockSpec(wq_t.shape, lambda b, i: (0, 0)),
                  pl.BlockSpec(keys.shape, lambda b, i: (0, 0, 0))],
        out_specs=[pl.BlockSpec((1, ts, d), lambda b, i: (b, i, 0)),
                   sel_spec, sel_spec, sel_spec, sel_spec],
        out_shape=[jax.ShapeDtypeStruct((bsz, seq, d), BF16),
                   sel_f32, sel_f32, sel_bf16, sel_bf16],
        scratch_shapes=[pltpu.VMEM((wq_t.shape[0], ts), F32)],
        compiler_params=_cparams(("parallel", "parallel")),
        name="peer_select",
    )(x1, norm_g, mod3, mod3, wq_t, keys)


def _gelu(x):
    return 0.5 * x * (1.0 + lax.erf(x * (1.0 / math.sqrt(2.0))))


def _peer_kernel(f_ref, u_ref, vt_ref, na_ref, ea_ref, rb_ref, eb_ref, x_ref, g2_ref, fg_ref,
                 o_ref, act_ref, p_ref, acc_ref, *, a_per_step):
    e = pl.program_id(2)
    tm = f_ref.shape[1]

    @pl.when(e == 0)
    def _():
        acc_ref[...] = jnp.zeros_like(acc_ref)

    nt = (((1,), (1,)), ((), ()))
    assert sum(PEER_SUBCHUNK_KEYS) == a_per_step
    first_key = [sum(PEER_SUBCHUNK_KEYS[:j]) for j in range(len(PEER_SUBCHUNK_KEYS))]
    zero = jnp.zeros((PEER_NKEYS, BF16_TILE_LANES), BF16)

    def rows_of(j):
        return pl.ds(first_key[j] * PEER_NKEYS, PEER_SUBCHUNK_KEYS[j] * PEER_NKEYS)

    def local_rows(j):
        return pl.ds(0, PEER_SUBCHUNK_KEYS[j] * PEER_NKEYS)

    def activations(j):
        act_ref[j % 2, local_rows(j), :] = lax.dot_general(
            u_ref[rows_of(j), :], f_ref[0], nt, preferred_element_type=F32)

    def gated(j):
        for lc in range(tm // BF16_TILE_LANES):
            cols = pl.ds(lc * BF16_TILE_LANES, BF16_TILE_LANES)
            for al in range(PEER_SUBCHUNK_KEYS[j]):
                a = first_key[j] + al
                w = zero
                for h in range(PEER_HEADS):
                    na = na_ref[h, pl.ds(a, 1), cols].astype(BF16)
                    ea = ea_ref[h, pl.ds(a, 1), cols].astype(BF16)
                    w = w + jnp.where(rb_ref[h, :, cols] < na, eb_ref[h, :, cols], zero) * ea
                rows = pl.ds(al * PEER_NKEYS, PEER_NKEYS)
                p_ref[j % 2, rows, cols] = _gelu(act_ref[j % 2, rows, cols]).astype(BF16) * w

    def values(j):
        acc_ref[...] += jnp.dot(vt_ref[:, rows_of(j)], p_ref[j % 2, local_rows(j), :],
                                preferred_element_type=F32)

    n_sub = len(PEER_SUBCHUNK_KEYS)
    for stage in range(n_sub + 2):
        if stage < n_sub:
            activations(stage)
        if 1 <= stage <= n_sub:
            gated(stage - 1)
        if stage >= 2:
            values(stage - 2)

    @pl.when(e == pl.num_programs(2) - 1)
    def _():
        y = x_ref[0] + g2_ref[0] * acc_ref[...].T
        ms = jnp.mean(y * y, axis=1, keepdims=True)
        o_ref[0] = y * lax.rsqrt(ms + EPS) * fg_ref[...]


def _peer_call(f, u, v_t, na, ea, rb, eb, x1, mod3, final_g, tm=512, ec=1024):
    bsz, seq, d = x1.shape
    experts = u.shape[0]
    nblk = seq // tm
    a_per_step = ec // PEER_NKEYS
    kern = functools.partial(_peer_kernel, a_per_step=a_per_step)
    tok = lambda b, i, e: b * nblk + i
    return pl.pallas_call(
        kern,
        grid=(bsz, nblk, experts // ec),
        in_specs=[pl.BlockSpec((1, tm, d), lambda b, i, e: (b, i, 0)),
                  pl.BlockSpec((ec, d), lambda b, i, e: (e, 0)),
                  pl.BlockSpec((d, ec), lambda b, i, e: (0, e)),
                  pl.BlockSpec((PEER_HEADS, a_per_step, tm), lambda b, i, e: (0, e, tok(b, i, e))),
                  pl.BlockSpec((PEER_HEADS, a_per_step, tm), lambda b, i, e: (0, e, tok(b, i, e))),
                  pl.BlockSpec((PEER_HEADS, PEER_NKEYS, tm), lambda b, i, e: (0, 0, tok(b, i, e))),
                  pl.BlockSpec((PEER_HEADS, PEER_NKEYS, tm), lambda b, i, e: (0, 0, tok(b, i, e))),
                  pl.BlockSpec((1, tm, d), lambda b, i, e: (b, i, 0)),
                  pl.BlockSpec((1, 1, d), lambda b, i, e: (b, 0, 5)),
                  pl.BlockSpec((1, d), lambda b, i, e: (0, 0))],
        out_specs=pl.BlockSpec((1, tm, d), lambda b, i, e: (b, i, 0)),
        out_shape=jax.ShapeDtypeStruct((bsz, seq, d), F32),
        scratch_shapes=[pltpu.VMEM((2, max(PEER_SUBCHUNK_KEYS) * PEER_NKEYS, tm), F32),
                        pltpu.VMEM((2, max(PEER_SUBCHUNK_KEYS) * PEER_NKEYS, tm), BF16),
                        pltpu.VMEM((d, tm), F32)],
        compiler_params=_cparams(("parallel", "parallel", "arbitrary")),
        name="peer_dense",
    )(f, u, v_t, na, ea, rb, eb, x1, mod3, final_g)


def _rope_tables(seq):
    pos = np.arange(seq)
    row = (pos // GRID_W).astype(np.float32)
    col = (pos % GRID_W).astype(np.float32)
    inv_freq = (ROPE_BASE ** (-np.arange(ROPE_HALF, dtype=np.float32) / ROPE_HALF)).astype(np.float32)
    lane = np.arange(LANES)
    axis = (lane % DIFF_QKDIM) // ROPE_AXIS_DIM
    freq = inv_freq[lane % ROPE_HALF]
    p = np.where(axis[None, :] == 0, row[:, None], col[:, None]).astype(np.float32)
    ang = p * freq[None, :]
    sign = np.where((lane % ROPE_AXIS_DIM) < ROPE_HALF, -1.0, 1.0).astype(np.float32)
    return jnp.asarray(np.cos(ang), F32), jnp.asarray(np.sin(ang) * sign[None, :], F32)


def kernel(x, c, ctx, c_ctx, ada_w, ada_b, norm1_g, w_in, pool_w, pool_b, pool_scale,
           diff_lambda, subln_g, w_out, norm2_g, peer_wq, peer_keys, peer_u, peer_v, final_g):
    bsz, seq, d = x.shape
    ctx_len = ctx.shape[1]
    layer = 0
    lam_init = 0.8 - 0.6 * math.exp(-0.3 * layer)
    mod_rows = 8
    assert bsz + 1 <= mod_rows

    cc = jnp.concatenate([c, c_ctx[None, :], jnp.zeros((mod_rows - bsz - 1, d), F32)], axis=0)
    mod = _ada_call(cc, ada_w[layer], ada_b[layer][None, :])
    mod3 = mod.reshape(mod_rows, 1, 6 * d)

    w_in_b = w_in[layer].astype(BF16)
    g1 = norm1_g[layer][None, :]
    cos, sin = _rope_tables(seq)
    in_width = w_in_b.shape[1]
    attn_width = (in_width - POOL_WIDTH) // 3
    z_lat = _inproj_call(x, g1, mod3, lambda b: b, w_in_b, cos, sin,
                         rope=True, col0=0, ncols=in_width)
    kv_ctx = _inproj_call(ctx, g1, mod3, lambda b: bsz, w_in_b, cos, sin,
                          rope=False, col0=POOL_WIDTH + attn_width, ncols=2 * attn_width)

    attn, u_b, v_t, wq_t, w_out_b = _attn_call(
        z_lat, kv_ctx, diff_lambda[layer], subln_g[layer][None, :],
        peer_u[layer], peer_v[layer], peer_wq[layer], w_out[layer], lam_init=lam_init)
    pool_y = _pool_call(z_lat, pool_w[layer], pool_b[layer], pool_scale[layer])
    x1 = _outproj_call(x, pool_y, attn, w_out_b, mod3)

    keys = peer_keys[layer].reshape(2 * PEER_HEADS, PEER_NKEYS, PEER_HALF)
    f, na, ea, rb, eb = _peer_sel_call(x1, norm2_g[layer][None, :], mod3, wq_t, keys)
    return _peer_call(f, u_b, v_t, na, ea, rb, eb, x1, mod3, final_g[None, :])
```

```python
import functools
import math

import jax
import jax.numpy as jnp
import numpy as np
from jax import lax
from jax.experimental import pallas as pl
from jax.experimental.pallas import tpu as pltpu

F32 = jnp.float32
BF16 = jnp.bfloat16

EPS = 1e-6
GRID_W = 64
POOL_WINDOWS = (2, 4, 8, 16)
POOL_GROUP_DIM = 128
POOL_WIDTH = POOL_GROUP_DIM * len(POOL_WINDOWS)
DIFF_VDIM = 128
DIFF_QKDIM = 64
ROPE_BASE = 10000.0
ROPE_AXIS_DIM = 32
ROPE_HALF = ROPE_AXIS_DIM // 2
PEER_HEADS = 8
PEER_NKEYS = 128
PEER_HALF = 128
PEER_TOPK = 16
LANES = 128
BF16_TILE_LANES = 256
SOFTMAX_ROWS = 16
POOL_PAD = 8
VMEM_LIMIT_BYTES = 56 * 1024 * 1024
NEG_INF = float("-inf")


def _cparams(sem):
    return pltpu.CompilerParams(dimension_semantics=sem, vmem_limit_bytes=VMEM_LIMIT_BYTES)


def _split_bf16(a):
    hi = a.astype(BF16)
    lo = (a - hi.astype(F32)).astype(BF16)
    return hi, lo


def _ada_kernel(c_ref, w_ref, b_ref, o_ref):
    c = c_ref[...]
    s = c * (1.0 / (1.0 + jnp.exp(-c)))
    o_ref[...] = jnp.dot(s.astype(BF16), w_ref[...].astype(BF16),
                         preferred_element_type=F32) + b_ref[...]


def _ada_call(cc, w, b, tn=1024):
    rows, d = cc.shape
    n = w.shape[1]
    return pl.pallas_call(
        _ada_kernel,
        grid=(n // tn,),
        in_specs=[pl.BlockSpec((rows, d), lambda j: (0, 0)),
                  pl.BlockSpec((d, tn), lambda j: (0, j)),
                  pl.BlockSpec((1, tn), lambda j: (0, j))],
        out_specs=pl.BlockSpec((rows, tn), lambda j: (0, j)),
        out_shape=jax.ShapeDtypeStruct((rows, n), F32),
        compiler_params=_cparams(("arbitrary",)),
        name="ada_mod",
    )(cc, w, b)


def _norm_mod(xf, g, sh, sc):
    ms = jnp.mean(xf * xf, axis=-1, keepdims=True)
    y = xf * lax.rsqrt(ms + EPS) * g
    return y * (1.0 + sc) + sh


def _inproj_kernel(x_ref, g_ref, sh_ref, sc_ref, w_ref, cos_ref, sin_ref, o_ref, h_ref, z_ref,
                   *, rope, col0, tn):
    n_tiles = o_ref.shape[2] // tn
    q_tiles = (DIFF_VDIM * 12) // tn
    pool_tiles = POOL_WIDTH // tn

    def matmul(n):
        z_ref[n % 2] = jnp.dot(h_ref[...], w_ref[:, pl.ds(col0 + n * tn, tn)],
                               preferred_element_type=F32)

    def epilogue(n):
        z = z_ref[n % 2]
        if rope and pool_tiles <= n < pool_tiles + 2 * q_tiles:
            reps = tn // LANES
            cos = jnp.concatenate([cos_ref[...]] * reps, axis=1)
            sin = jnp.concatenate([sin_ref[...]] * reps, axis=1)
            lane = lax.broadcasted_iota(jnp.int32, z.shape, 1)
            first = (lane % ROPE_AXIS_DIM) < ROPE_HALF
            partner = jnp.where(first, pltpu.roll(z, tn - ROPE_HALF, 1),
                                pltpu.roll(z, ROPE_HALF, 1))
            z = z * cos + partner * sin
            if n < pool_tiles + q_tiles:
                z = z * DIFF_QKDIM ** -0.5
        o_ref[0, :, pl.ds(n * tn, tn)] = z.astype(BF16)

    h = _norm_mod(x_ref[0], g_ref[...], sh_ref[0], sc_ref[0])
    h_ref[...] = h.astype(BF16)
    for stage in range(n_tiles + 1):
        if stage < n_tiles:
            matmul(stage)
        if stage >= 1:
            epilogue(stage - 1)


def _inproj_call(x, g, mod3, mod_row_fn, w, cos, sin, *, rope, col0, ncols, tm=512, tn=512):
    bsz, rows, d = x.shape
    tm = min(tm, rows)
    kern = functools.partial(_inproj_kernel, rope=rope, col0=col0, tn=tn)
    return pl.pallas_call(
        kern,
        grid=(bsz, rows // tm),
        in_specs=[pl.BlockSpec((1, tm, d), lambda b, i: (b, i, 0)),
                  pl.BlockSpec((1, d), lambda b, i: (0, 0)),
                  pl.BlockSpec((1, 1, d), lambda b, i: (mod_row_fn(b), 0, 0)),
                  pl.BlockSpec((1, 1, d), lambda b, i: (mod_row_fn(b), 0, 1)),
                  pl.BlockSpec(w.shape, lambda b, i: (0, 0), pipeline_mode=pl.Buffered(1)),
                  pl.BlockSpec((tm, LANES), lambda b, i: (i, 0)),
                  pl.BlockSpec((tm, LANES), lambda b, i: (i, 0))],
        out_specs=pl.BlockSpec((1, tm, ncols), lambda b, i: (b, i, 0)),
        out_shape=jax.ShapeDtypeStruct((bsz, rows, ncols), BF16),
        scratch_shapes=[pltpu.VMEM((tm, d), BF16),
                        pltpu.VMEM((2, tm, tn), F32)],
        compiler_params=_cparams(("parallel", "arbitrary")),
        name="inproj_rope" if rope else "inproj_ctx",
    )(x, g, mod3, mod3, w, cos, sin)


def _attn_kernel(q_ref, kc_ref, vc_ref, kl_ref, vl_ref, dl_ref, sg_ref, u_ref, v_ref, wq_ref,
                 wo_ref, o_ref, ub_ref, vt_ref, wqt_ref, wob_ref, s_ref, e_ref, va_ref,
                 *, lam_init, sub, side_heads):
    ctx_len = kc_ref.shape[1]
    seq = kl_ref.shape[1]
    n_sub = q_ref.shape[1] // sub
    lq = dl_ref[...]
    lam = (jnp.exp(jnp.sum(lq[0:1] * lq[1:2], axis=1, keepdims=True))
           - jnp.exp(jnp.sum(lq[2:3] * lq[3:4], axis=1, keepdims=True)) + lam_init)
    nt = (((1,), (1,)), ((), ()))
    lane = lax.broadcasted_iota(jnp.int32, (sub, LANES), 1)
    zero = jnp.zeros((sub, LANES), BF16)

    ones_col = jnp.where(lax.broadcasted_iota(jnp.int32, (ctx_len + seq, LANES), 1) == 0,
                         1.0, 0.0).astype(BF16)
    va_ref[pl.ds(0, ctx_len), pl.ds(0, LANES)] = vc_ref[0]
    va_ref[pl.ds(ctx_len, seq), pl.ds(0, LANES)] = vl_ref[0]
    va_ref[:, pl.ds(LANES, LANES)] = ones_col

    def scores(t):
        q = q_ref[0, pl.ds(t * sub, sub), :]
        q2 = jnp.concatenate([jnp.where(lane < DIFF_QKDIM, q, zero),
                              jnp.where(lane >= DIFF_QKDIM, q, zero)], axis=0)
        s_ref[t % 2, :, pl.ds(0, ctx_len)] = lax.dot_general(
            q2, kc_ref[0], nt, preferred_element_type=F32)
        s_ref[t % 2, :, pl.ds(ctx_len, seq)] = lax.dot_general(
            q2, kl_ref[0], nt, preferred_element_type=F32)

    def numerators(t):
        for r in range(0, 2 * sub, SOFTMAX_ROWS):
            rows = pl.ds(r, SOFTMAX_ROWS)
            s = s_ref[t % 2, rows, :]
            e_ref[t % 2, rows, :] = jnp.exp(s - jnp.max(s, axis=1, keepdims=True)).astype(BF16)

    def outputs(t):
        oa = jnp.dot(e_ref[t % 2], va_ref[...], preferred_element_type=F32)
        o2 = oa[:, :LANES] / oa[:, LANES:LANES + 1]
        o = o2[:sub] - lam * o2[sub:]
        ms = jnp.mean(o * o, axis=1, keepdims=True)
        y = o * lax.rsqrt(ms + EPS) * sg_ref[...]
        o_ref[0, pl.ds(t * sub, sub), :] = (y * (1.0 - lam_init)).astype(BF16)

    for stage in range(n_sub + 2):
        if stage < n_sub:
            scores(stage)
        if 1 <= stage <= n_sub:
            numerators(stage - 1)
        if stage >= 2:
            outputs(stage - 2)

    @pl.when(jnp.logical_and(pl.program_id(1) < side_heads, pl.program_id(2) == 0))
    def _():
        ub_ref[...] = u_ref[...].astype(BF16)
        vt_ref[...] = v_ref[...].T.astype(BF16)

    @pl.when(jnp.logical_and(pl.program_id(1) >= side_heads, pl.program_id(2) == 0))
    def _():
        wqt_ref[...] = wq_ref[...].T.astype(BF16)
        wob_ref[...] = wo_ref[...].astype(BF16)


def _attn_call(z_lat, kv_ctx, diff_lambda, subln_g, peer_u, peer_v, peer_wq, w_out, *, lam_init,
               tq=2048, sub=256):
    bsz, seq, _ = z_lat.shape
    ctx_len = kv_ctx.shape[1]
    tq = min(tq, seq)
    heads = 12
    qb = POOL_WIDTH // LANES
    kb = qb + heads
    vb = kb + heads
    experts, d = peer_u.shape
    side_heads = 8
    trows = experts // (bsz * side_heads)
    assert trows * bsz * side_heads == experts
    tblk = lambda b, h, i: b * side_heads + jnp.minimum(h, side_heads - 1)
    rest = heads - side_heads
    wrows = peer_wq.shape[0] // (bsz * rest)
    assert wrows * bsz * rest == peer_wq.shape[0] == w_out.shape[0] and wrows % LANES == 0
    wblk = lambda b, h, i: b * rest + jnp.maximum(h - side_heads, 0)
    kern = functools.partial(_attn_kernel, lam_init=lam_init, sub=sub, side_heads=side_heads)
    return pl.pallas_call(
        kern,
        grid=(bsz, heads, seq // tq),
        in_specs=[pl.BlockSpec((1, tq, LANES), lambda b, h, i: (b, i, qb + h)),
                  pl.BlockSpec((1, ctx_len, LANES), lambda b, h, i: (b, 0, h)),
                  pl.BlockSpec((1, ctx_len, LANES), lambda b, h, i: (b, 0, heads + h)),
                  pl.BlockSpec((1, seq, LANES), lambda b, h, i: (b, 0, kb + h)),
                  pl.BlockSpec((1, seq, LANES), lambda b, h, i: (b, 0, vb + h)),
                  pl.BlockSpec(diff_lambda.shape, lambda b, h, i: (0, 0)),
                  pl.BlockSpec((1, LANES), lambda b, h, i: (0, 0)),
                  pl.BlockSpec((trows, d), lambda b, h, i: (tblk(b, h, i), 0)),
                  pl.BlockSpec((trows, d), lambda b, h, i: (tblk(b, h, i), 0)),
                  pl.BlockSpec((wrows, peer_wq.shape[1]), lambda b, h, i: (wblk(b, h, i), 0)),
                  pl.BlockSpec((wrows, w_out.shape[1]), lambda b, h, i: (wblk(b, h, i), 0))],
        out_specs=[pl.BlockSpec((1, tq, LANES), lambda b, h, i: (b, i, h)),
                   pl.BlockSpec((trows, d), lambda b, h, i: (tblk(b, h, i), 0)),
                   pl.BlockSpec((d, trows), lambda b, h, i: (0, tblk(b, h, i))),
                   pl.BlockSpec((peer_wq.shape[1], wrows), lambda b, h, i: (0, wblk(b, h, i))),
                   pl.BlockSpec((wrows, w_out.shape[1]), lambda b, h, i: (wblk(b, h, i), 0))],
        out_shape=[jax.ShapeDtypeStruct((bsz, seq, heads * DIFF_VDIM), BF16),
                   jax.ShapeDtypeStruct((experts, d), BF16),
                   jax.ShapeDtypeStruct((d, experts), BF16),
                   jax.ShapeDtypeStruct(peer_wq.shape[::-1], BF16),
                   jax.ShapeDtypeStruct(w_out.shape, BF16)],
        scratch_shapes=[pltpu.VMEM((2, 2 * sub, ctx_len + seq), F32),
                        pltpu.VMEM((2, 2 * sub, ctx_len + seq), BF16),
                        pltpu.VMEM((ctx_len + seq, 2 * LANES), BF16)],
        compiler_params=_cparams(("parallel", "arbitrary", "arbitrary")),
        name="diff_attn",
    )(z_lat, kv_ctx, kv_ctx, z_lat, z_lat, diff_lambda, subln_g, peer_u, peer_v, peer_wq, w_out)


def _pool_kernel(z_ref, w_ref, b_ref, s_ref, o_ref, zp_ref):
    seq = z_ref.shape[1]
    zp_ref[...] = jnp.zeros_like(zp_ref)
    zp_ref[pl.ds(POOL_PAD, seq), :] = z_ref[0].astype(F32)
    t = lax.broadcasted_iota(jnp.int32, (seq, POOL_GROUP_DIM), 0)
    outs = []
    for g, win in enumerate(POOL_WINDOWS):
        half = win // 2
        cols = pl.ds(g * POOL_GROUP_DIM, POOL_GROUP_DIM)
        acc = jnp.zeros((seq, POOL_GROUP_DIM), F32)
        for k in range(-half, half):
            acc = acc + zp_ref[pl.ds(POOL_PAD + k, seq), cols]
        cnt = (jnp.minimum(t + half, seq) - jnp.maximum(t - half, 0)).astype(F32)
        y = acc / cnt - zp_ref[pl.ds(POOL_PAD, seq), cols]
        r = jnp.dot(y.astype(BF16), w_ref[g].astype(BF16), preferred_element_type=F32)
        outs.append((r + b_ref[g]) * s_ref[g])
    o_ref[0] = jnp.concatenate(outs, axis=1).astype(BF16)


def _pool_call(z_lat, pool_w, pool_b, pool_scale):
    bsz, seq, _ = z_lat.shape
    ng = len(POOL_WINDOWS)
    return pl.pallas_call(
        _pool_kernel,
        grid=(bsz,),
        in_specs=[pl.BlockSpec((1, seq, POOL_WIDTH), lambda b: (b, 0, 0)),
                  pl.BlockSpec((ng, POOL_GROUP_DIM, POOL_GROUP_DIM), lambda b: (0, 0, 0)),
                  pl.BlockSpec((ng, 1, POOL_GROUP_DIM), lambda b: (0, 0, 0)),
                  pl.BlockSpec((ng, 1, POOL_GROUP_DIM), lambda b: (0, 0, 0))],
        out_specs=pl.BlockSpec((1, seq, POOL_WIDTH), lambda b: (b, 0, 0)),
        out_shape=jax.ShapeDtypeStruct((bsz, seq, POOL_WIDTH), BF16),
        scratch_shapes=[pltpu.VMEM((seq + 2 * POOL_PAD, POOL_WIDTH), F32)],
        compiler_params=_cparams(("parallel",)),
        name="pool_mix",
    )(z_lat, pool_w, pool_b.reshape(ng, 1, POOL_GROUP_DIM),
      pool_scale.reshape(ng, 1, POOL_GROUP_DIM))


def _outproj_kernel(x_ref, p_ref, a_ref, w_ref, g_ref, o_ref):
    r = (jnp.dot(p_ref[0], w_ref[pl.ds(0, POOL_WIDTH), :], preferred_element_type=F32)
         + jnp.dot(a_ref[0], w_ref[pl.ds(POOL_WIDTH, a_ref.shape[2]), :],
                   preferred_element_type=F32))
    o_ref[0] = x_ref[0] + g_ref[0] * r


def _outproj_call(x, pool_y, attn, w_out, mod3, tm=256):
    bsz, seq, d = x.shape
    aw = attn.shape[2]
    return pl.pallas_call(
        _outproj_kernel,
        grid=(bsz, seq // tm),
        in_specs=[pl.BlockSpec((1, tm, d), lambda b, i: (b, i, 0)),
                  pl.BlockSpec((1, tm, POOL_WIDTH), lambda b, i: (b, i, 0)),
                  pl.BlockSpec((1, tm, aw), lambda b, i: (b, i, 0)),
                  pl.BlockSpec(w_out.shape, lambda b, i: (0, 0)),
                  pl.BlockSpec((1, 1, d), lambda b, i: (b, 0, 2))],
        out_specs=pl.BlockSpec((1, tm, d), lambda b, i: (b, i, 0)),
        out_shape=jax.ShapeDtypeStruct((bsz, seq, d), F32),
        compiler_params=_cparams(("parallel", "parallel")),
        name="outproj_residual",
    )(x, pool_y, attn, w_out, mod3)


def _top16_rows(s):
    n, t = s.shape
    row = lax.broadcasted_iota(jnp.int32, (n, t), 0).astype(F32)
    row16 = lax.broadcasted_iota(jnp.int32, (PEER_TOPK, t), 0)
    work = s
    rank = jnp.full((n, t), 127.0, F32)
    vals = jnp.zeros((PEER_TOPK, t), F32)
    for i in range(PEER_TOPK):
        m = jnp.max(work, axis=0, keepdims=True)
        idx = jnp.min(jnp.where(work == m, row, float(n)), axis=0, keepdims=True)
        sel = row == idx
        rank = jnp.where(sel, float(i), rank)
        vals = jnp.where(row16 == i, m, vals)
        work = jnp.where(sel, NEG_INF, work)
    return vals, rank


_CAND_SLABS = ((0, 0, 8), (0, 8, 8), (1, 0, 8), (2, 0, 5), (3, 0, 4), (4, 0, 3),
               (5, 0, 2), (6, 0, 2), (7, 0, 2))


def _pair_select(va, vb):
    t = va.shape[1]
    sub = lax.broadcasted_iota(jnp.int32, (8, t), 0)
    subf = sub.astype(F32)
    cands, flats = [], []
    for (i, j0, lim) in _CAND_SLABS:
        c = va[i:i + 1, :] + vb[j0:j0 + 8, :]
        cands.append(jnp.where(sub < lim, c, NEG_INF))
        flats.append(subf + float(i * PEER_TOPK + j0))
    cands.append(va[8:16, :] + vb[0:1, :])
    flats.append(subf * float(PEER_TOPK) + float(8 * PEER_TOPK))
    cand = jnp.concatenate(cands, axis=0)
    flat = jnp.concatenate(flats, axis=0)
    work = cand
    big = float(PEER_TOPK * PEER_TOPK)
    for _ in range(PEER_TOPK):
        m = jnp.max(work, axis=0, keepdims=True)
        idx = jnp.min(jnp.where(work == m, flat, big), axis=0, keepdims=True)
        work = jnp.where(flat == idx, NEG_INF, work)
    sel = jnp.logical_and(work == NEG_INF, cand > NEG_INF)
    self32 = sel.astype(F32)
    top = va[0:1, :] + vb[0:1, :]
    z = jnp.sum(jnp.where(sel, jnp.exp(cand - top), 0.0), axis=0, keepdims=True)
    counts = [jnp.sum(self32[0:16], axis=0, keepdims=True)]
    for k in range(2, 9):
        counts.append(jnp.sum(self32[8 * k:8 * k + 8], axis=0, keepdims=True))
    n = jnp.concatenate(counts + [self32[72:80]], axis=0)
    return n, z


def _sort_network(n):
    size = 16
    pairs = []
    p = 1
    while p < size:
        k = p
        while k >= 1:
            for j in range(k % p, size - k, 2 * k):
                for i in range(min(k, size - j - k)):
                    if (i + j) // (2 * p) == (i + j + k) // (2 * p):
                        pairs.append((i + j, i + j + k))
            k //= 2
        p *= 2
    return [(i, j) for (i, j) in pairs if j < n]


def _sorted_top16(slabs):
    s = list(slabs)
    depth = len(s)
    for i, j in _sort_network(depth):
        s[i], s[j] = jnp.maximum(s[i], s[j]), jnp.minimum(s[i], s[j])
    t = s[0].shape[1]
    sub = lax.broadcasted_iota(jnp.int32, (8, t), 0).astype(F32)
    rows = []
    for i in range(PEER_TOPK):
        m = jnp.max(s[0], axis=0, keepdims=True)
        rows.append(m)
        if i == PEER_TOPK - 1:
            break
        first = jnp.min(jnp.where(s[0] == m, sub, 8.0), axis=0, keepdims=True)
        win = sub == first
        for r in range(min(depth, PEER_TOPK - 1 - i)):
            s[r] = jnp.where(win, s[r + 1] if r + 1 < depth else NEG_INF, s[r])
    return rows


def _tie_rows(s, v):
    cnt = jnp.sum(jnp.where(s >= v[PEER_TOPK - 1], 1.0, 0.0), axis=0, keepdims=True)
    tie = jnp.where(cnt != float(PEER_TOPK), 1.0, 0.0)
    for i in range(PEER_TOPK - 1):
        tie = tie + jnp.where(v[i] == v[i + 1], 1.0, 0.0)
    return tie


def _pair_select_fast(va, vb):
    t = va[0].shape[1]
    row16 = lax.broadcasted_iota(jnp.int32, (PEER_TOPK, t), 0)
    sub = lax.broadcasted_iota(jnp.int32, (8, t), 0)
    va_arr = jnp.zeros((PEER_TOPK, t), F32)
    vb_arr = jnp.zeros((PEER_TOPK, t), F32)
    for i in range(PEER_TOPK):
        va_arr = jnp.where(row16 == i, va[i], va_arr)
        vb_arr = jnp.where(row16 == i, vb[i], vb_arr)
    cands = []
    for (i, j0, lim) in _CAND_SLABS:
        cands.append(jnp.where(sub < lim, va[i] + vb_arr[j0:j0 + 8, :], NEG_INF))
    cands.append(va_arr[8:16, :] + vb[0])
    tau = _sorted_top16(cands)[PEER_TOPK - 1]
    top = va[0] + vb[0]
    sel = [jnp.where(c >= tau, 1.0, 0.0) for c in cands]
    z = sum(jnp.sum(s * jnp.exp(c - top), axis=0, keepdims=True) for s, c in zip(sel, cands))
    n = [jnp.sum(sel[0] + sel[1], axis=0, keepdims=True)]
    for k in range(2, 9):
        n.append(jnp.sum(sel[k], axis=0, keepdims=True))
    for r in range(8):
        n.append(sel[9][r:r + 1, :])
    total = sum(n[:8]) + jnp.sum(sel[9], axis=0, keepdims=True)
    return n, z, jnp.where(total != float(PEER_TOPK), 1.0, 0.0)


def _peer_sel_kernel(x_ref, g_ref, sh_ref, sc_ref, wq_ref, keys_ref,
                     f_ref, na_ref, ea_ref, rb_ref, eb_ref, qt_ref):
    f = _norm_mod(x_ref[0], g_ref[...], sh_ref[0], sc_ref[0]).astype(BF16)
    f_ref[0] = f
    nt = (((1,), (1,)), ((), ()))
    qt_ref[...] = lax.dot_general(wq_ref[...], f, nt, preferred_element_type=F32)

    def scores(hp):
        k_hi, k_lo = _split_bf16(keys_ref[hp])
        q_hi, q_lo = _split_bf16(qt_ref[pl.ds(pl.multiple_of(hp * PEER_HALF, PEER_HALF),
                                              PEER_HALF), :])
        return (jnp.dot(k_hi, q_hi, preferred_element_type=F32)
                + jnp.dot(k_hi, q_lo, preferred_element_type=F32)
                + jnp.dot(k_lo, q_hi, preferred_element_type=F32))

    def head(h, carry):
        sa = scores(2 * h)
        sb = scores(2 * h + 1)

        va = _sorted_top16([sa[8 * k:8 * k + 8] for k in range(PEER_NKEYS // 8)])
        vb = _sorted_top16([sb[8 * k:8 * k + 8] for k in range(PEER_NKEYS // 8)])
        n_rows, z, tie2 = _pair_select_fast(va, vb)
        na = jnp.zeros_like(sa)
        rank_b = jnp.zeros_like(sb)
        for i in range(PEER_TOPK):
            na = jnp.where(sa == va[i], n_rows[i], na)
            rank_b = rank_b + jnp.where(vb[i] > sb, 1.0, 0.0)
        na_ref[h] = na
        ea_ref[h] = jnp.exp(sa - va[0]) / z
        rb_ref[h] = rank_b.astype(BF16)
        eb_ref[h] = jnp.exp(sb - vb[0]).astype(BF16)

        tie = tie2 + _tie_rows(sa, va) + _tie_rows(sb, vb)

        @pl.when(jnp.max(tie) > 0.0)
        def _():
            va_t, rank_a_t = _top16_rows(sa)
            vb_t, rank_b_t = _top16_rows(sb)
            n_t, z_t = _pair_select(va_t, vb_t)
            na_t = jnp.zeros_like(sa)
            for i in range(PEER_TOPK):
                na_t = jnp.where(rank_a_t == float(i), n_t[i:i + 1, :], na_t)
            na_ref[h] = na_t
            ea_ref[h] = jnp.exp(sa - va_t[0:1, :]) / z_t
            rb_ref[h] = rank_b_t.astype(BF16)
            eb_ref[h] = jnp.exp(sb - vb_t[0:1, :]).astype(BF16)

        return carry

    lax.fori_loop(0, PEER_HEADS, head, 0)


def _peer_sel_call(x1, norm_g, mod3, wq_t, keys, ts=512):
    bsz, seq, d = x1.shape
    tokens = bsz * seq
    nblk = seq // ts
    sel_f32 = jax.ShapeDtypeStruct((PEER_HEADS, PEER_NKEYS, tokens), F32)
    sel_bf16 = jax.ShapeDtypeStruct((PEER_HEADS, PEER_NKEYS, tokens), BF16)
    sel_spec = pl.BlockSpec((PEER_HEADS, PEER_NKEYS, ts), lambda b, i: (0, 0, b * nblk + i))
    return pl.pallas_call(
        _peer_sel_kernel,
        grid=(bsz, nblk),
        in_specs=[pl.BlockSpec((1, ts, d), lambda b, i: (b, i, 0)),
                  pl.BlockSpec((1, d), lambda b, i: (0, 0)),
                  pl.BlockSpec((1, 1, d), lambda b, i: (b, 0, 3)),
                  pl.BlockSpec((1, 1, d), lambda b, i: (b, 0, 4)),
                  pl.BlockSpec(wq_t.shape, lambda b, i: (0, 0)),
                  pl.BlockSpec(keys.shape, lambda b, i: (0, 0, 0))],
        out_specs=[pl.BlockSpec((1, ts, d), lambda b, i: (b, i, 0)),
                   sel_spec, sel_spec, sel_spec, sel_spec],
        out_shape=[jax.ShapeDtypeStruct((bsz, seq, d), BF16),
                   sel_f32, sel_f32, sel_bf16, sel_bf16],
        scratch_shapes=[pltpu.VMEM((wq_t.shape[0], ts), F32)],
        compiler_params=_cparams(("parallel", "parallel")),
        name="peer_select",
    )(x1, norm_g, mod3, mod3, wq_t, keys)


def _gelu(x):
    return 0.5 * x * (1.0 + lax.erf(x * (1.0 / math.sqrt(2.0))))


def _peer_kernel(f_ref, u_ref, vt_ref, na_ref, ea_ref, rb_ref, eb_ref,
                 x_ref, g2_ref, fg_ref, o_ref, act_ref, p_ref, acc_ref):
    e = pl.program_id(2)
    last = pl.num_programs(2) - 1
    tm = f_ref.shape[1]
    sub = u_ref.shape[0] // 2
    keys_per_sub = sub // PEER_NKEYS
    nt = (((1,), (1,)), ((), ()))
    zero = jnp.zeros((PEER_NKEYS, BF16_TILE_LANES), BF16)

    def activations(j):
        act_ref[j] = lax.dot_general(u_ref[pl.ds(j * sub, sub), :], f_ref[0], nt,
                                     preferred_element_type=F32)

    def gated(j, n_ref, g_ref, slot):
        for lc in range(tm // BF16_TILE_LANES):
            cols = pl.ds(lc * BF16_TILE_LANES, BF16_TILE_LANES)
            for al in range(keys_per_sub):
                a = j * keys_per_sub + al
                w = zero
                for h in range(PEER_HEADS):
                    na = n_ref[h, pl.ds(a, 1), cols].astype(BF16)
                    ea = g_ref[h, pl.ds(a, 1), cols].astype(BF16)
                    w = w + jnp.where(rb_ref[h, :, cols] < na, eb_ref[h, :, cols], zero) * ea
                rows = pl.ds(al * PEER_NKEYS, PEER_NKEYS)
                g = _gelu(act_ref[j, rows, cols]).astype(BF16)
                p_ref[slot, pl.ds(j * sub + al * PEER_NKEYS, PEER_NKEYS), cols] = g * w

    def values(slot):
        acc_ref[...] += jnp.dot(vt_ref[...], p_ref[slot], preferred_element_type=F32)

    this = e % 2
    before = 1 - this

    @pl.when(e == 0)
    def _():
        acc_ref[...] = jnp.zeros_like(acc_ref)
        activations(0)
        activations(1)
        gated(0, na_ref, ea_ref, this)
        gated(1, na_ref, ea_ref, this)

    @pl.when(jnp.logical_and(e > 0, e < last))
    def _():
        activations(0)
        activations(1)
        values(before)
        gated(0, na_ref, ea_ref, this)
        gated(1, na_ref, ea_ref, this)

    @pl.when(e == last)
    def _():
        values(before)
        y = x_ref[0] + g2_ref[0] * acc_ref[...].T
        ms = jnp.mean(y * y, axis=1, keepdims=True)
        o_ref[0] = y * lax.rsqrt(ms + EPS) * fg_ref[...]


def _peer_call(f, u, v_t, na, ea, rb, eb, x1, mod3, final_g, tm=512, ec=1024):
    bsz, seq, d = x1.shape
    experts = u.shape[0]
    nblk = seq // tm
    nchunk = experts // ec
    a_per_step = ec // PEER_NKEYS
    tok = lambda b, i, e: b * nblk + i
    cur = lambda e: jnp.minimum(e, nchunk - 1)
    prev = lambda e: jnp.maximum(e - 1, 0)
    sel_rows = lambda which: pl.BlockSpec(
        (PEER_HEADS, a_per_step, tm), lambda b, i, e: (0, which(e), tok(b, i, e)))
    sel_all = pl.BlockSpec((PEER_HEADS, PEER_NKEYS, tm), lambda b, i, e: (0, 0, tok(b, i, e)))
    return pl.pallas_call(
        _peer_kernel,
        grid=(bsz, nblk, nchunk + 1),
        in_specs=[pl.BlockSpec((1, tm, d), lambda b, i, e: (b, i, 0)),
                  pl.BlockSpec((ec, d), lambda b, i, e: (cur(e), 0)),
                  pl.BlockSpec((d, ec), lambda b, i, e: (0, prev(e))),
                  sel_rows(cur), sel_rows(cur), sel_all, sel_all,
                  pl.BlockSpec((1, tm, d), lambda b, i, e: (b, i, 0)),
                  pl.BlockSpec((1, 1, d), lambda b, i, e: (b, 0, 5)),
                  pl.BlockSpec((1, d), lambda b, i, e: (0, 0))],
        out_specs=pl.BlockSpec((1, tm, d), lambda b, i, e: (b, i, 0)),
        out_shape=jax.ShapeDtypeStruct((bsz, seq, d), F32),
        scratch_shapes=[pltpu.VMEM((2, ec // 2, tm), F32),
                        pltpu.VMEM((2, ec, tm), BF16),
                        pltpu.VMEM((d, tm), F32)],
        compiler_params=_cparams(("parallel", "parallel", "arbitrary")),
        name="peer_dense",
    )(f, u, v_t, na, ea, rb, eb, x1, mod3, final_g)


def _rope_tables(seq):
    pos = np.arange(seq)
    row = (pos // GRID_W).astype(np.float32)
    col = (pos % GRID_W).astype(np.float32)
    inv_freq = (ROPE_BASE ** (-np.arange(ROPE_HALF, dtype=np.float32) / ROPE_HALF)).astype(np.float32)
    lane = np.arange(LANES)
    axis = (lane % DIFF_QKDIM) // ROPE_AXIS_DIM
    freq = inv_freq[lane % ROPE_HALF]
    p = np.where(axis[None, :] == 0, row[:, None], col[:, None]).astype(np.float32)
    ang = p * freq[None, :]
    sign = np.where((lane % ROPE_AXIS_DIM) < ROPE_HALF, -1.0, 1.0).astype(np.float32)
    return jnp.asarray(np.cos(ang), F32), jnp.asarray(np.sin(ang) * sign[None, :], F32)


def kernel(x, c, ctx, c_ctx, ada_w, ada_b, norm1_g, w_in, pool_w, pool_b, pool_scale,
           diff_lambda, subln_g, w_out, norm2_g, peer_wq, peer_keys, peer_u, peer_v, final_g):
    bsz, seq, d = x.shape
    ctx_len = ctx.shape[1]
    layer = 0
    lam_init = 0.8 - 0.6 * math.exp(-0.3 * layer)
    mod_rows = 8
    assert bsz + 1 <= mod_rows

    cc = jnp.concatenate([c, c_ctx[None, :], jnp.zeros((mod_rows - bsz - 1, d), F32)], axis=0)
    mod = _ada_call(cc, ada_w[layer], ada_b[layer][None, :])
    mod3 = mod.reshape(mod_rows, 1, 6 * d)

    w_in_b = w_in[layer].astype(BF16)
    g1 = norm1_g[layer][None, :]
    cos, sin = _rope_tables(seq)
    in_width = w_in_b.shape[1]
    attn_width = (in_width - POOL_WIDTH) // 3
    z_lat = _inproj_call(x, g1, mod3, lambda b: b, w_in_b, cos, sin,
                         rope=True, col0=0, ncols=in_width)
    kv_ctx = _inproj_call(ctx, g1, mod3, lambda b: bsz, w_in_b, cos, sin,
                          rope=False, col0=POOL_WIDTH + attn_width, ncols=2 * attn_width)

    attn, u_b, v_t, wq_t, w_out_b = _attn_call(
        z_lat, kv_ctx, diff_lambda[layer], subln_g[layer][None, :],
        peer_u[layer], peer_v[layer], peer_wq[layer], w_out[layer], lam_init=lam_init)
    pool_y = _pool_call(z_lat, pool_w[layer], pool_b[layer], pool_scale[layer])
    x1 = _outproj_call(x, pool_y, attn, w_out_b, mod3)

    keys = peer_keys[layer].reshape(2 * PEER_HEADS, PEER_NKEYS, PEER_HALF)
    f, na, ea, rb, eb = _peer_sel_call(x1, norm2_g[layer][None, :], mod3, wq_t, keys)
    return _peer_call(f, u_b, v_t, na, ea, rb, eb, x1, mod3, final_g[None, :])
```

```python
import functools
import math

import jax
import jax.numpy as jnp
import numpy as np
from jax import lax
from jax.experimental import pallas as pl
from jax.experimental.pallas import tpu as pltpu

F32 = jnp.float32
BF16 = jnp.bfloat16

EPS = 1e-6
GRID_W = 64
POOL_WINDOWS = (2, 4, 8, 16)
POOL_GROUP_DIM = 128
POOL_WIDTH = POOL_GROUP_DIM * len(POOL_WINDOWS)
DIFF_VDIM = 128
DIFF_QKDIM = 64
ROPE_BASE = 10000.0
ROPE_AXIS_DIM = 32
ROPE_HALF = ROPE_AXIS_DIM // 2
PEER_HEADS = 8
PEER_NKEYS = 128
PEER_HALF = 128
PEER_TOPK = 16
LANES = 128
BF16_TILE_LANES = 256
SOFTMAX_ROWS = 16
POOL_PAD = 8
VMEM_LIMIT_BYTES = 56 * 1024 * 1024
NEG_INF = float("-inf")


def _cparams(sem):
    return pltpu.CompilerParams(dimension_semantics=sem, vmem_limit_bytes=VMEM_LIMIT_BYTES)


def _split_bf16(a):
    hi = a.astype(BF16)
    lo = (a - hi.astype(F32)).astype(BF16)
    return hi, lo


def _ada_kernel(c_ref, w_ref, b_ref, o_ref):
    c = c_ref[...]
    s = c * (1.0 / (1.0 + jnp.exp(-c)))
    o_ref[...] = jnp.dot(s.astype(BF16), w_ref[...].astype(BF16),
                         preferred_element_type=F32) + b_ref[...]


def _ada_call(cc, w, b, tn=1024):
    rows, d = cc.shape
    n = w.shape[1]
    return pl.pallas_call(
        _ada_kernel,
        grid=(n // tn,),
        in_specs=[pl.BlockSpec((rows, d), lambda j: (0, 0)),
                  pl.BlockSpec((d, tn), lambda j: (0, j)),
                  pl.BlockSpec((1, tn), lambda j: (0, j))],
        out_specs=pl.BlockSpec((rows, tn), lambda j: (0, j)),
        out_shape=jax.ShapeDtypeStruct((rows, n), F32),
        compiler_params=_cparams(("arbitrary",)),
        name="ada_mod",
    )(cc, w, b)


def _norm_mod(xf, g, sh, sc):
    ms = jnp.mean(xf * xf, axis=-1, keepdims=True)
    y = xf * lax.rsqrt(ms + EPS) * g
    return y * (1.0 + sc) + sh


def _inproj_kernel(x_ref, g_ref, sh_ref, sc_ref, w_ref, cos_ref, sin_ref, o_ref, h_ref, z_ref,
                   *, rope, col0, tn):
    n_tiles = o_ref.shape[2] // tn
    q_tiles = (DIFF_VDIM * 12) // tn
    pool_tiles = POOL_WIDTH // tn

    def matmul(n):
        z_ref[n % 2] = jnp.dot(h_ref[...], w_ref[:, pl.ds(col0 + n * tn, tn)],
                               preferred_element_type=F32)

    def epilogue(n):
        z = z_ref[n % 2]
        if rope and pool_tiles <= n < pool_tiles + 2 * q_tiles:
            reps = tn // LANES
            cos = jnp.concatenate([cos_ref[...]] * reps, axis=1)
            sin = jnp.concatenate([sin_ref[...]] * reps, axis=1)
            lane = lax.broadcasted_iota(jnp.int32, z.shape, 1)
            first = (lane % ROPE_AXIS_DIM) < ROPE_HALF
            partner = jnp.where(first, pltpu.roll(z, tn - ROPE_HALF, 1),
                                pltpu.roll(z, ROPE_HALF, 1))
            z = z * cos + partner * sin
            if n < pool_tiles + q_tiles:
                z = z * DIFF_QKDIM ** -0.5
        o_ref[0, :, pl.ds(n * tn, tn)] = z.astype(BF16)

    h = _norm_mod(x_ref[0], g_ref[...], sh_ref[0], sc_ref[0])
    h_ref[...] = h.astype(BF16)
    for stage in range(n_tiles + 1):
        if stage < n_tiles:
            matmul(stage)
        if stage >= 1:
            epilogue(stage - 1)


def _inproj_call(x, g, mod3, mod_row_fn, w, cos, sin, *, rope, col0, ncols, tm=512, tn=512):
    bsz, rows, d = x.shape
    tm = min(tm, rows)
    kern = functools.partial(_inproj_kernel, rope=rope, col0=col0, tn=tn)
    return pl.pallas_call(
        kern,
        grid=(bsz, rows // tm),
        in_specs=[pl.BlockSpec((1, tm, d), lambda b, i: (b, i, 0)),
                  pl.BlockSpec((1, d), lambda b, i: (0, 0)),
                  pl.BlockSpec((1, 1, d), lambda b, i: (mod_row_fn(b), 0, 0)),
                  pl.BlockSpec((1, 1, d), lambda b, i: (mod_row_fn(b), 0, 1)),
                  pl.BlockSpec(w.shape, lambda b, i: (0, 0), pipeline_mode=pl.Buffered(1)),
                  pl.BlockSpec((tm, LANES), lambda b, i: (i, 0)),
                  pl.BlockSpec((tm, LANES), lambda b, i: (i, 0))],
        out_specs=pl.BlockSpec((1, tm, ncols), lambda b, i: (b, i, 0)),
        out_shape=jax.ShapeDtypeStruct((bsz, rows, ncols), BF16),
        scratch_shapes=[pltpu.VMEM((tm, d), BF16),
                        pltpu.VMEM((2, tm, tn), F32)],
        compiler_params=_cparams(("parallel", "arbitrary")),
        name="inproj_rope" if rope else "inproj_ctx",
    )(x, g, mod3, mod3, w, cos, sin)


def _attn_kernel(q_ref, kc_ref, vc_ref, kl_ref, vl_ref, dl_ref, sg_ref, u_ref, v_ref, wq_ref,
                 wo_ref, o_ref, ub_ref, vt_ref, wqt_ref, wob_ref, s_ref, e_ref, va_ref,
                 *, lam_init, sub, side_heads):
    ctx_len = kc_ref.shape[1]
    seq = kl_ref.shape[1]
    n_sub = q_ref.shape[1] // sub
    lq = dl_ref[...]
    lam = (jnp.exp(jnp.sum(lq[0:1] * lq[1:2], axis=1, keepdims=True))
           - jnp.exp(jnp.sum(lq[2:3] * lq[3:4], axis=1, keepdims=True)) + lam_init)
    nt = (((1,), (1,)), ((), ()))
    lane = lax.broadcasted_iota(jnp.int32, (sub, LANES), 1)
    zero = jnp.zeros((sub, LANES), BF16)

    ones_col = jnp.where(lax.broadcasted_iota(jnp.int32, (ctx_len + seq, LANES), 1) == 0,
                         1.0, 0.0).astype(BF16)
    va_ref[pl.ds(0, ctx_len), pl.ds(0, LANES)] = vc_ref[0]
    va_ref[pl.ds(ctx_len, seq), pl.ds(0, LANES)] = vl_ref[0]
    va_ref[:, pl.ds(LANES, LANES)] = ones_col

    def scores(t):
        q = q_ref[0, pl.ds(t * sub, sub), :]
        q2 = jnp.concatenate([jnp.where(lane < DIFF_QKDIM, q, zero),
                              jnp.where(lane >= DIFF_QKDIM, q, zero)], axis=0)
        s_ref[t % 2, :, pl.ds(0, ctx_len)] = lax.dot_general(
            q2, kc_ref[0], nt, preferred_element_type=F32)
        s_ref[t % 2, :, pl.ds(ctx_len, seq)] = lax.dot_general(
            q2, kl_ref[0], nt, preferred_element_type=F32)

    def numerators(t):
        for r in range(0, 2 * sub, SOFTMAX_ROWS):
            rows = pl.ds(r, SOFTMAX_ROWS)
            s = s_ref[t % 2, rows, :]
            e_ref[t % 2, rows, :] = jnp.exp(s - jnp.max(s, axis=1, keepdims=True)).astype(BF16)

    def outputs(t):
        oa = jnp.dot(e_ref[t % 2], va_ref[...], preferred_element_type=F32)
        o2 = oa[:, :LANES] / oa[:, LANES:LANES + 1]
        o = o2[:sub] - lam * o2[sub:]
        ms = jnp.mean(o * o, axis=1, keepdims=True)
        y = o * lax.rsqrt(ms + EPS) * sg_ref[...]
        o_ref[0, pl.ds(t * sub, sub), :] = (y * (1.0 - lam_init)).astype(BF16)

    for stage in range(n_sub + 2):
        if stage < n_sub:
            scores(stage)
        if 1 <= stage <= n_sub:
            numerators(stage - 1)
        if stage >= 2:
            outputs(stage - 2)

    @pl.when(jnp.logical_and(pl.program_id(1) < side_heads, pl.program_id(2) == 0))
    def _():
        ub_ref[...] = u_ref[...].astype(BF16)
        vt_ref[...] = v_ref[...].T.astype(BF16)

    @pl.when(jnp.logical_and(pl.program_id(1) >= side_heads, pl.program_id(2) == 0))
    def _():
        wqt_ref[...] = wq_ref[...].T.astype(BF16)
        wob_ref[...] = wo_ref[...].astype(BF16)


def _attn_call(z_lat, kv_ctx, diff_lambda, subln_g, peer_u, peer_v, peer_wq, w_out, *, lam_init,
               tq=2048, sub=256):
    bsz, seq, _ = z_lat.shape
    ctx_len = kv_ctx.shape[1]
    tq = min(tq, seq)
    heads = 12
    qb = POOL_WIDTH // LANES
    kb = qb + heads
    vb = kb + heads
    experts, d = peer_u.shape
    side_heads = 8
    trows = experts // (bsz * side_heads)
    assert trows * bsz * side_heads == experts
    tblk = lambda b, h, i: b * side_heads + jnp.minimum(h, side_heads - 1)
    rest = heads - side_heads
    wrows = peer_wq.shape[0] // (bsz * rest)
    assert wrows * bsz * rest == peer_wq.shape[0] == w_out.shape[0] and wrows % LANES == 0
    wblk = lambda b, h, i: b * rest + jnp.maximum(h - side_heads, 0)
    kern = functools.partial(_attn_kernel, lam_init=lam_init, sub=sub, side_heads=side_heads)
    return pl.pallas_call(
        kern,
        grid=(bsz, heads, seq // tq),
        in_specs=[pl.BlockSpec((1, tq, LANES), lambda b, h, i: (b, i, qb + h)),
                  pl.BlockSpec((1, ctx_len, LANES), lambda b, h, i: (b, 0, h)),
                  pl.BlockSpec((1, ctx_len, LANES), lambda b, h, i: (b, 0, heads + h)),
                  pl.BlockSpec((1, seq, LANES), lambda b, h, i: (b, 0, kb + h)),
                  pl.BlockSpec((1, seq, LANES), lambda b, h, i: (b, 0, vb + h)),
                  pl.BlockSpec(diff_lambda.shape, lambda b, h, i: (0, 0)),
                  pl.BlockSpec((1, LANES), lambda b, h, i: (0, 0)),
                  pl.BlockSpec((trows, d), lambda b, h, i: (tblk(b, h, i), 0)),
                  pl.BlockSpec((trows, d), lambda b, h, i: (tblk(b, h, i), 0)),
                  pl.BlockSpec((wrows, peer_wq.shape[1]), lambda b, h, i: (wblk(b, h, i), 0)),
                  pl.BlockSpec((wrows, w_out.shape[1]), lambda b, h, i: (wblk(b, h, i), 0))],
        out_specs=[pl.BlockSpec((1, tq, LANES), lambda b, h, i: (b, i, h)),
                   pl.BlockSpec((trows, d), lambda b, h, i: (tblk(b, h, i), 0)),
                   pl.BlockSpec((d, trows), lambda b, h, i: (0, tblk(b, h, i))),
                   pl.BlockSpec((peer_wq.shape[1], wrows), lambda b, h, i: (0, wblk(b, h, i))),
                   pl.BlockSpec((wrows, w_out.shape[1]), lambda b, h, i: (wblk(b, h, i), 0))],
        out_shape=[jax.ShapeDtypeStruct((bsz, seq, heads * DIFF_VDIM), BF16),
                   jax.ShapeDtypeStruct((experts, d), BF16),
                   jax.ShapeDtypeStruct((d, experts), BF16),
                   jax.ShapeDtypeStruct(peer_wq.shape[::-1], BF16),
                   jax.ShapeDtypeStruct(w_out.shape, BF16)],
        scratch_shapes=[pltpu.VMEM((2, 2 * sub, ctx_len + seq), F32),
                        pltpu.VMEM((2, 2 * sub, ctx_len + seq), BF16),
                        pltpu.VMEM((ctx_len + seq, 2 * LANES), BF16)],
        compiler_params=_cparams(("parallel", "arbitrary", "arbitrary")),
        name="diff_attn",
    )(z_lat, kv_ctx, kv_ctx, z_lat, z_lat, diff_lambda, subln_g, peer_u, peer_v, peer_wq, w_out)


def _pool_kernel(z_ref, w_ref, b_ref, s_ref, o_ref, zp_ref):
    seq = z_ref.shape[1]
    zp_ref[...] = jnp.zeros_like(zp_ref)
    zp_ref[pl.ds(POOL_PAD, seq), :] = z_ref[0].astype(F32)
    t = lax.broadcasted_iota(jnp.int32, (seq, POOL_GROUP_DIM), 0)
    outs = []
    for g, win in enumerate(POOL_WINDOWS):
        half = win // 2
        cols = pl.ds(g * POOL_GROUP_DIM, POOL_GROUP_DIM)
        acc = jnp.zeros((seq, POOL_GROUP_DIM), F32)
        for k in range(-half, half):
            acc = acc + zp_ref[pl.ds(POOL_PAD + k, seq), cols]
        cnt = (jnp.minimum(t + half, seq) - jnp.maximum(t - half, 0)).astype(F32)
        y = acc / cnt - zp_ref[pl.ds(POOL_PAD, seq), cols]
        r = jnp.dot(y.astype(BF16), w_ref[g].astype(BF16), preferred_element_type=F32)
        outs.append((r + b_ref[g]) * s_ref[g])
    o_ref[0] = jnp.concatenate(outs, axis=1).astype(BF16)


def _pool_call(z_lat, pool_w, pool_b, pool_scale):
    bsz, seq, _ = z_lat.shape
    ng = len(POOL_WINDOWS)
    return pl.pallas_call(
        _pool_kernel,
        grid=(bsz,),
        in_specs=[pl.BlockSpec((1, seq, POOL_WIDTH), lambda b: (b, 0, 0)),
                  pl.BlockSpec((ng, POOL_GROUP_DIM, POOL_GROUP_DIM), lambda b: (0, 0, 0)),
                  pl.BlockSpec((ng, 1, POOL_GROUP_DIM), lambda b: (0, 0, 0)),
                  pl.BlockSpec((ng, 1, POOL_GROUP_DIM), lambda b: (0, 0, 0))],
        out_specs=pl.BlockSpec((1, seq, POOL_WIDTH), lambda b: (b, 0, 0)),
        out_shape=jax.ShapeDtypeStruct((bsz, seq, POOL_WIDTH), BF16),
        scratch_shapes=[pltpu.VMEM((seq + 2 * POOL_PAD, POOL_WIDTH), F32)],
        compiler_params=_cparams(("parallel",)),
        name="pool_mix",
    )(z_lat, pool_w, pool_b.reshape(ng, 1, POOL_GROUP_DIM),
      pool_scale.reshape(ng, 1, POOL_GROUP_DIM))


def _outproj_kernel(x_ref, p_ref, a_ref, w_ref, g_ref, o_ref):
    r = (jnp.dot(p_ref[0], w_ref[pl.ds(0, POOL_WIDTH), :], preferred_element_type=F32)
         + jnp.dot(a_ref[0], w_ref[pl.ds(POOL_WIDTH, a_ref.shape[2]), :],
                   preferred_element_type=F32))
    o_ref[0] = x_ref[0] + g_ref[0] * r


def _outproj_call(x, pool_y, attn, w_out, mod3, tm=256):
    bsz, seq, d = x.shape
    aw = attn.shape[2]
    return pl.pallas_call(
        _outproj_kernel,
        grid=(bsz, seq // tm),
        in_specs=[pl.BlockSpec((1, tm, d), lambda b, i: (b, i, 0)),
                  pl.BlockSpec((1, tm, POOL_WIDTH), lambda b, i: (b, i, 0)),
                  pl.BlockSpec((1, tm, aw), lambda b, i: (b, i, 0)),
                  pl.BlockSpec(w_out.shape, lambda b, i: (0, 0)),
                  pl.BlockSpec((1, 1, d), lambda b, i: (b, 0, 2))],
        out_specs=pl.BlockSpec((1, tm, d), lambda b, i: (b, i, 0)),
        out_shape=jax.ShapeDtypeStruct((bsz, seq, d), F32),
        compiler_params=_cparams(("parallel", "parallel")),
        name="outproj_residual",
    )(x, pool_y, attn, w_out, mod3)


def _top16_rows(s):
    n, t = s.shape
    row = lax.broadcasted_iota(jnp.int32, (n, t), 0).astype(F32)
    row16 = lax.broadcasted_iota(jnp.int32, (PEER_TOPK, t), 0)
    work = s
    rank = jnp.full((n, t), 127.0, F32)
    vals = jnp.zeros((PEER_TOPK, t), F32)
    for i in range(PEER_TOPK):
        m = jnp.max(work, axis=0, keepdims=True)
        idx = jnp.min(jnp.where(work == m, row, float(n)), axis=0, keepdims=True)
        sel = row == idx
        rank = jnp.where(sel, float(i), rank)
        vals = jnp.where(row16 == i, m, vals)
        work = jnp.where(sel, NEG_INF, work)
    return vals, rank


_CAND_SLABS = ((0, 0, 8), (0, 8, 8), (1, 0, 8), (2, 0, 5), (3, 0, 4), (4, 0, 3),
               (5, 0, 2), (6, 0, 2), (7, 0, 2))


def _pair_select(va, vb):
    t = va.shape[1]
    sub = lax.broadcasted_iota(jnp.int32, (8, t), 0)
    subf = sub.astype(F32)
    cands, flats = [], []
    for (i, j0, lim) in _CAND_SLABS:
        c = va[i:i + 1, :] + vb[j0:j0 + 8, :]
        cands.append(jnp.where(sub < lim, c, NEG_INF))
        flats.append(subf + float(i * PEER_TOPK + j0))
    cands.append(va[8:16, :] + vb[0:1, :])
    flats.append(subf * float(PEER_TOPK) + float(8 * PEER_TOPK))
    cand = jnp.concatenate(cands, axis=0)
    flat = jnp.concatenate(flats, axis=0)
    work = cand
    big = float(PEER_TOPK * PEER_TOPK)
    for _ in range(PEER_TOPK):
        m = jnp.max(work, axis=0, keepdims=True)
        idx = jnp.min(jnp.where(work == m, flat, big), axis=0, keepdims=True)
        work = jnp.where(flat == idx, NEG_INF, work)
    sel = jnp.logical_and(work == NEG_INF, cand > NEG_INF)
    self32 = sel.astype(F32)
    top = va[0:1, :] + vb[0:1, :]
    z = jnp.sum(jnp.where(sel, jnp.exp(cand - top), 0.0), axis=0, keepdims=True)
    counts = [jnp.sum(self32[0:16], axis=0, keepdims=True)]
    for k in range(2, 9):
        counts.append(jnp.sum(self32[8 * k:8 * k + 8], axis=0, keepdims=True))
    n = jnp.concatenate(counts + [self32[72:80]], axis=0)
    return n, z


def _sort_network(n):
    size = 16
    pairs = []
    p = 1
    while p < size:
        k = p
        while k >= 1:
            for j in range(k % p, size - k, 2 * k):
                for i in range(min(k, size - j - k)):
                    if (i + j) // (2 * p) == (i + j + k) // (2 * p):
                        pairs.append((i + j, i + j + k))
            k //= 2
        p *= 2
    return [(i, j) for (i, j) in pairs if j < n]


def _sorted_top16(slabs):
    s = list(slabs)
    depth = len(s)
    for i, j in _sort_network(depth):
        s[i], s[j] = jnp.maximum(s[i], s[j]), jnp.minimum(s[i], s[j])
    t = s[0].shape[1]
    sub = lax.broadcasted_iota(jnp.int32, (8, t), 0).astype(F32)
    rows = []
    for i in range(PEER_TOPK):
        m = jnp.max(s[0], axis=0, keepdims=True)
        rows.append(m)
        if i == PEER_TOPK - 1:
            break
        first = jnp.min(jnp.where(s[0] == m, sub, 8.0), axis=0, keepdims=True)
        win = sub == first
        for r in range(min(depth, PEER_TOPK - 1 - i)):
            s[r] = jnp.where(win, s[r + 1] if r + 1 < depth else NEG_INF, s[r])
    return rows


def _tie_rows(s, v):
    cnt = jnp.sum(jnp.where(s >= v[PEER_TOPK - 1], 1.0, 0.0), axis=0, keepdims=True)
    tie = jnp.where(cnt != float(PEER_TOPK), 1.0, 0.0)
    for i in range(PEER_TOPK - 1):
        tie = tie + jnp.where(v[i] == v[i + 1], 1.0, 0.0)
    return tie


def _pair_select_fast(va, vb):
    t = va[0].shape[1]
    row16 = lax.broadcasted_iota(jnp.int32, (PEER_TOPK, t), 0)
    sub = lax.broadcasted_iota(jnp.int32, (8, t), 0)
    va_arr = jnp.zeros((PEER_TOPK, t), F32)
    vb_arr = jnp.zeros((PEER_TOPK, t), F32)
    for i in range(PEER_TOPK):
        va_arr = jnp.where(row16 == i, va[i], va_arr)
        vb_arr = jnp.where(row16 == i, vb[i], vb_arr)
    cands = []
    for (i, j0, lim) in _CAND_SLABS:
        cands.append(jnp.where(sub < lim, va[i] + vb_arr[j0:j0 + 8, :], NEG_INF))
    cands.append(va_arr[8:16, :] + vb[0])
    tau = _sorted_top16(cands)[PEER_TOPK - 1]
    top = va[0] + vb[0]
    sel = [jnp.where(c >= tau, 1.0, 0.0) for c in cands]
    z = sum(jnp.sum(s * jnp.exp(c - top), axis=0, keepdims=True) for s, c in zip(sel, cands))
    n = [jnp.sum(sel[0] + sel[1], axis=0, keepdims=True)]
    for k in range(2, 9):
        n.append(jnp.sum(sel[k], axis=0, keepdims=True))
    for r in range(8):
        n.append(sel[9][r:r + 1, :])
    total = sum(n[:8]) + jnp.sum(sel[9], axis=0, keepdims=True)
    return n, z, jnp.where(total != float(PEER_TOPK), 1.0, 0.0)


def _peer_sel_kernel(x_ref, g_ref, sh_ref, sc_ref, wq_ref, keys_ref,
                     f_ref, na_ref, ea_ref, rb_ref, eb_ref, qt_ref):
    ft = _norm_mod(x_ref[0], g_ref[...], sh_ref[0], sc_ref[0]).T.astype(BF16)
    f_ref[...] = ft
    qt_ref[...] = jnp.dot(wq_ref[...], ft, preferred_element_type=F32)

    def scores(hp):
        k_hi, k_lo = _split_bf16(keys_ref[hp])
        q_hi, q_lo = _split_bf16(qt_ref[pl.ds(pl.multiple_of(hp * PEER_HALF, PEER_HALF),
                                              PEER_HALF), :])
        return (jnp.dot(k_hi, q_hi, preferred_element_type=F32)
                + jnp.dot(k_hi, q_lo, preferred_element_type=F32)
                + jnp.dot(k_lo, q_hi, preferred_element_type=F32))

    def head(h, carry):
        sa = scores(2 * h)
        sb = scores(2 * h + 1)

        va = _sorted_top16([sa[8 * k:8 * k + 8] for k in range(PEER_NKEYS // 8)])
        vb = _sorted_top16([sb[8 * k:8 * k + 8] for k in range(PEER_NKEYS // 8)])
        n_rows, z, tie2 = _pair_select_fast(va, vb)
        na = jnp.zeros_like(sa)
        rank_b = jnp.zeros_like(sb)
        for i in range(PEER_TOPK):
            na = jnp.where(sa == va[i], n_rows[i], na)
            rank_b = rank_b + jnp.where(vb[i] > sb, 1.0, 0.0)
        na_ref[h] = na
        ea_ref[h] = jnp.exp(sa - va[0]) / z
        rb_ref[h] = rank_b.astype(BF16)
        eb_ref[h] = jnp.exp(sb - vb[0]).astype(BF16)

        tie = tie2 + _tie_rows(sa, va) + _tie_rows(sb, vb)

        @pl.when(jnp.max(tie) > 0.0)
        def _():
            va_t, rank_a_t = _top16_rows(sa)
            vb_t, rank_b_t = _top16_rows(sb)
            n_t, z_t = _pair_select(va_t, vb_t)
            na_t = jnp.zeros_like(sa)
            for i in range(PEER_TOPK):
                na_t = jnp.where(rank_a_t == float(i), n_t[i:i + 1, :], na_t)
            na_ref[h] = na_t
            ea_ref[h] = jnp.exp(sa - va_t[0:1, :]) / z_t
            rb_ref[h] = rank_b_t.astype(BF16)
            eb_ref[h] = jnp.exp(sb - vb_t[0:1, :]).astype(BF16)

        return carry

    lax.fori_loop(0, PEER_HEADS, head, 0)


def _peer_sel_call(x1, norm_g, mod3, wq_t, keys, ts=512):
    bsz, seq, d = x1.shape
    tokens = bsz * seq
    nblk = seq // ts
    sel_f32 = jax.ShapeDtypeStruct((PEER_HEADS, PEER_NKEYS, tokens), F32)
    sel_bf16 = jax.ShapeDtypeStruct((PEER_HEADS, PEER_NKEYS, tokens), BF16)
    sel_spec = pl.BlockSpec((PEER_HEADS, PEER_NKEYS, ts), lambda b, i: (0, 0, b * nblk + i))
    return pl.pallas_call(
        _peer_sel_kernel,
        grid=(bsz, nblk),
        in_specs=[pl.BlockSpec((1, ts, d), lambda b, i: (b, i, 0)),
                  pl.BlockSpec((1, d), lambda b, i: (0, 0)),
                  pl.BlockSpec((1, 1, d), lambda b, i: (b, 0, 3)),
                  pl.BlockSpec((1, 1, d), lambda b, i: (b, 0, 4)),
                  pl.BlockSpec(wq_t.shape, lambda b, i: (0, 0)),
                  pl.BlockSpec(keys.shape, lambda b, i: (0, 0, 0))],
        out_specs=[pl.BlockSpec((d, ts), lambda b, i: (0, b * nblk + i)),
                   sel_spec, sel_spec, sel_spec, sel_spec],
        out_shape=[jax.ShapeDtypeStruct((d, tokens), BF16),
                   sel_f32, sel_f32, sel_bf16, sel_bf16],
        scratch_shapes=[pltpu.VMEM((wq_t.shape[0], ts), F32)],
        compiler_params=_cparams(("parallel", "parallel")),
        name="peer_select",
    )(x1, norm_g, mod3, mod3, wq_t, keys)


def _gelu(x):
    return 0.5 * x * (1.0 + lax.erf(x * (1.0 / math.sqrt(2.0))))


def _peer_kernel(f_ref, u_ref, vt_ref, na_ref, ea_ref, rb_ref, eb_ref,
                 x_ref, g2_ref, fg_ref, o_ref, act_ref, p_ref, acc_ref):
    e = pl.program_id(2)
    last = pl.num_programs(2) - 1
    tm = f_ref.shape[1]
    sub = u_ref.shape[0] // 2
    keys_per_sub = sub // PEER_NKEYS
    zero = jnp.zeros((PEER_NKEYS, BF16_TILE_LANES), BF16)

    def activations(j):
        act_ref[j] = jnp.dot(u_ref[pl.ds(j * sub, sub), :], f_ref[...],
                             preferred_element_type=F32)

    def gated(j, n_ref, g_ref, slot):
        for lc in range(tm // BF16_TILE_LANES):
            cols = pl.ds(lc * BF16_TILE_LANES, BF16_TILE_LANES)
            for al in range(keys_per_sub):
                a = j * keys_per_sub + al
                w = zero
                for h in range(PEER_HEADS):
                    na = n_ref[h, pl.ds(a, 1), cols].astype(BF16)
                    ea = g_ref[h, pl.ds(a, 1), cols].astype(BF16)
                    w = w + jnp.where(rb_ref[h, :, cols] < na, eb_ref[h, :, cols], zero) * ea
                rows = pl.ds(al * PEER_NKEYS, PEER_NKEYS)
                g = _gelu(act_ref[j, rows, cols]).astype(BF16)
                p_ref[slot, pl.ds(j * sub + al * PEER_NKEYS, PEER_NKEYS), cols] = g * w

    def values(slot):
        acc_ref[...] += jnp.dot(vt_ref[...], p_ref[slot], preferred_element_type=F32)

    this = e % 2
    before = 1 - this

    @pl.when(e == 0)
    def _():
        acc_ref[...] = jnp.zeros_like(acc_ref)
        activations(0)
        activations(1)
        gated(0, na_ref, ea_ref, this)
        gated(1, na_ref, ea_ref, this)

    @pl.when(jnp.logical_and(e > 0, e < last))
    def _():
        activations(0)
        activations(1)
        values(before)
        gated(0, na_ref, ea_ref, this)
        gated(1, na_ref, ea_ref, this)

    @pl.when(e == last)
    def _():
        values(before)
        y = x_ref[0] + g2_ref[0] * acc_ref[...].T
        ms = jnp.mean(y * y, axis=1, keepdims=True)
        o_ref[0] = y * lax.rsqrt(ms + EPS) * fg_ref[...]


def _peer_call(f, u, v_t, na, ea, rb, eb, x1, mod3, final_g, tm=512, ec=1024):
    bsz, seq, d = x1.shape
    experts = u.shape[0]
    nblk = seq // tm
    nchunk = experts // ec
    a_per_step = ec // PEER_NKEYS
    tok = lambda b, i, e: b * nblk + i
    cur = lambda e: jnp.minimum(e, nchunk - 1)
    prev = lambda e: jnp.maximum(e - 1, 0)
    sel_rows = lambda which: pl.BlockSpec(
        (PEER_HEADS, a_per_step, tm), lambda b, i, e: (0, which(e), tok(b, i, e)))
    sel_all = pl.BlockSpec((PEER_HEADS, PEER_NKEYS, tm), lambda b, i, e: (0, 0, tok(b, i, e)))
    return pl.pallas_call(
        _peer_kernel,
        grid=(bsz, nblk, nchunk + 1),
        in_specs=[pl.BlockSpec((d, tm), lambda b, i, e: (0, tok(b, i, e))),
                  pl.BlockSpec((ec, d), lambda b, i, e: (cur(e), 0)),
                  pl.BlockSpec((d, ec), lambda b, i, e: (0, prev(e))),
                  sel_rows(cur), sel_rows(cur), sel_all, sel_all,
                  pl.BlockSpec((1, tm, d), lambda b, i, e: (b, i, 0)),
                  pl.BlockSpec((1, 1, d), lambda b, i, e: (b, 0, 5)),
                  pl.BlockSpec((1, d), lambda b, i, e: (0, 0))],
        out_specs=pl.BlockSpec((1, tm, d), lambda b, i, e: (b, i, 0)),
        out_shape=jax.ShapeDtypeStruct((bsz, seq, d), F32),
        scratch_shapes=[pltpu.VMEM((2, ec // 2, tm), F32),
                        pltpu.VMEM((2, ec, tm), BF16),
                        pltpu.VMEM((d, tm), F32)],
        compiler_params=_cparams(("parallel", "parallel", "arbitrary")),
        name="peer_dense",
    )(f, u, v_t, na, ea, rb, eb, x1, mod3, final_g)


def _rope_tables(seq):
    pos = np.arange(seq)
    row = (pos // GRID_W).astype(np.float32)
    col = (pos % GRID_W).astype(np.float32)
    inv_freq = (ROPE_BASE ** (-np.arange(ROPE_HALF, dtype=np.float32) / ROPE_HALF)).astype(np.float32)
    lane = np.arange(LANES)
    axis = (lane % DIFF_QKDIM) // ROPE_AXIS_DIM
    freq = inv_freq[lane % ROPE_HALF]
    p = np.where(axis[None, :] == 0, row[:, None], col[:, None]).astype(np.float32)
    ang = p * freq[None, :]
    sign = np.where((lane % ROPE_AXIS_DIM) < ROPE_HALF, -1.0, 1.0).astype(np.float32)
    return jnp.asarray(np.cos(ang), F32), jnp.asarray(np.sin(ang) * sign[None, :], F32)


def kernel(x, c, ctx, c_ctx, ada_w, ada_b, norm1_g, w_in, pool_w, pool_b, pool_scale,
           diff_lambda, subln_g, w_out, norm2_g, peer_wq, peer_keys, peer_u, peer_v, final_g):
    bsz, seq, d = x.shape
    ctx_len = ctx.shape[1]
    layer = 0
    lam_init = 0.8 - 0.6 * math.exp(-0.3 * layer)
    mod_rows = 8
    assert bsz + 1 <= mod_rows

    cc = jnp.concatenate([c, c_ctx[None, :], jnp.zeros((mod_rows - bsz - 1, d), F32)], axis=0)
    mod = _ada_call(cc, ada_w[layer], ada_b[layer][None, :])
    mod3 = mod.reshape(mod_rows, 1, 6 * d)

    w_in_b = w_in[layer].astype(BF16)
    g1 = norm1_g[layer][None, :]
    cos, sin = _rope_tables(seq)
    in_width = w_in_b.shape[1]
    attn_width = (in_width - POOL_WIDTH) // 3
    z_lat = _inproj_call(x, g1, mod3, lambda b: b, w_in_b, cos, sin,
                         rope=True, col0=0, ncols=in_width)
    kv_ctx = _inproj_call(ctx, g1, mod3, lambda b: bsz, w_in_b, cos, sin,
                          rope=False, col0=POOL_WIDTH + attn_width, ncols=2 * attn_width)

    attn, u_b, v_t, wq_t, w_out_b = _attn_call(
        z_lat, kv_ctx, diff_lambda[layer], subln_g[layer][None, :],
        peer_u[layer], peer_v[layer], peer_wq[layer], w_out[layer], lam_init=lam_init)
    pool_y = _pool_call(z_lat, pool_w[layer], pool_b[layer], pool_scale[layer])
    x1 = _outproj_call(x, pool_y, attn, w_out_b, mod3)

    keys = peer_keys[layer].reshape(2 * PEER_HEADS, PEER_NKEYS, PEER_HALF)
    f, na, ea, rb, eb = _peer_sel_call(x1, norm2_g[layer][None, :], mod3, wq_t, keys)
    return _peer_call(f, u_b, v_t, na, ea, rb, eb, x1, mod3, final_g[None, :])
```

```python
import functools
import math

import jax
import jax.numpy as jnp
import numpy as np
from jax import lax
from jax.experimental import pallas as pl
from jax.experimental.pallas import tpu as pltpu

F32 = jnp.float32
BF16 = jnp.bfloat16

EPS = 1e-6
GRID_W = 64
POOL_WINDOWS = (2, 4, 8, 16)
POOL_GROUP_DIM = 128
POOL_WIDTH = POOL_GROUP_DIM * len(POOL_WINDOWS)
DIFF_VDIM = 128
DIFF_HEADS = 12
DIFF_QKDIM = 64
ROPE_BASE = 10000.0
ROPE_AXIS_DIM = 32
ROPE_HALF = ROPE_AXIS_DIM // 2
PEER_HEADS = 8
PEER_NKEYS = 128
PEER_HALF = 128
PEER_TOPK = 16
LANES = 128
BF16_TILE_LANES = 256
SOFTMAX_ROWS = 16
POOL_PAD = 8
VMEM_LIMIT_BYTES = 56 * 1024 * 1024
NEG_INF = float("-inf")


def _cparams(sem):
    return pltpu.CompilerParams(dimension_semantics=sem, vmem_limit_bytes=VMEM_LIMIT_BYTES)


def _split_bf16(a):
    hi = a.astype(BF16)
    lo = (a - hi.astype(F32)).astype(BF16)
    return hi, lo


def _ada_kernel(c_ref, w_ref, b_ref, o_ref):
    c = c_ref[...]
    s = c * (1.0 / (1.0 + jnp.exp(-c)))
    o_ref[...] = jnp.dot(s.astype(BF16), w_ref[...].astype(BF16),
                         preferred_element_type=F32) + b_ref[...]


def _ada_call(cc, w, b, tn=1024):
    rows, d = cc.shape
    n = w.shape[1]
    return pl.pallas_call(
        _ada_kernel,
        grid=(n // tn,),
        in_specs=[pl.BlockSpec((rows, d), lambda j: (0, 0)),
                  pl.BlockSpec((d, tn), lambda j: (0, j)),
                  pl.BlockSpec((1, tn), lambda j: (0, j))],
        out_specs=pl.BlockSpec((rows, tn), lambda j: (0, j)),
        out_shape=jax.ShapeDtypeStruct((rows, n), F32),
        compiler_params=_cparams(("arbitrary",)),
        name="ada_mod",
    )(cc, w, b)


def _norm_mod(xf, g, sh, sc):
    ms = jnp.mean(xf * xf, axis=-1, keepdims=True)
    y = xf * lax.rsqrt(ms + EPS) * g
    return y * (1.0 + sc) + sh


def _inproj_kernel(x_ref, g_ref, sh_ref, sc_ref, w_ref, cos_ref, sin_ref, o_ref, h_ref, z_ref,
                   *, rope, col0, tn):
    n_tiles = o_ref.shape[2] // tn
    q_tiles = (DIFF_VDIM * DIFF_HEADS) // tn
    pool_tiles = POOL_WIDTH // tn

    def matmul(n):
        z_ref[n % 2] = jnp.dot(h_ref[...], w_ref[:, pl.ds(col0 + n * tn, tn)],
                               preferred_element_type=F32)

    def epilogue(n):
        z = z_ref[n % 2]
        if rope and pool_tiles <= n < pool_tiles + 2 * q_tiles:
            reps = tn // LANES
            cos = jnp.concatenate([cos_ref[...]] * reps, axis=1)
            sin = jnp.concatenate([sin_ref[...]] * reps, axis=1)
            lane = lax.broadcasted_iota(jnp.int32, z.shape, 1)
            first = (lane % ROPE_AXIS_DIM) < ROPE_HALF
            partner = jnp.where(first, pltpu.roll(z, tn - ROPE_HALF, 1),
                                pltpu.roll(z, ROPE_HALF, 1))
            z = z * cos + partner * sin
            if n < pool_tiles + q_tiles:
                z = z * DIFF_QKDIM ** -0.5
        o_ref[0, :, pl.ds(n * tn, tn)] = z.astype(BF16)

    h = _norm_mod(x_ref[0], g_ref[...], sh_ref[0], sc_ref[0])
    h_ref[...] = h.astype(BF16)
    for stage in range(n_tiles + 1):
        if stage < n_tiles:
            matmul(stage)
        if stage >= 1:
            epilogue(stage - 1)


def _inproj_call(x, g, mod3, mod_row_fn, w, cos, sin, *, rope, col0, ncols, tm=512, tn=512):
    bsz, rows, d = x.shape
    tm = min(tm, rows)
    kern = functools.partial(_inproj_kernel, rope=rope, col0=col0, tn=tn)
    return pl.pallas_call(
        kern,
        grid=(bsz, rows // tm),
        in_specs=[pl.BlockSpec((1, tm, d), lambda b, i: (b, i, 0)),
                  pl.BlockSpec((1, d), lambda b, i: (0, 0)),
                  pl.BlockSpec((1, 1, d), lambda b, i: (mod_row_fn(b), 0, 0)),
                  pl.BlockSpec((1, 1, d), lambda b, i: (mod_row_fn(b), 0, 1)),
                  pl.BlockSpec(w.shape, lambda b, i: (0, 0), pipeline_mode=pl.Buffered(1)),
                  pl.BlockSpec((tm, LANES), lambda b, i: (i, 0)),
                  pl.BlockSpec((tm, LANES), lambda b, i: (i, 0))],
        out_specs=pl.BlockSpec((1, tm, ncols), lambda b, i: (b, i, 0)),
        out_shape=jax.ShapeDtypeStruct((bsz, rows, ncols), BF16),
        scratch_shapes=[pltpu.VMEM((tm, d), BF16),
                        pltpu.VMEM((2, tm, tn), F32)],
        compiler_params=_cparams(("parallel", "arbitrary")),
        name="inproj_rope" if rope else "inproj_ctx",
    )(x, g, mod3, mod3, w, cos, sin)


def _attn_kernel(q_ref, kc_ref, vc_ref, kl_ref, vl_ref, dl_ref, sg_ref, u_ref, v_ref, wq_ref,
                 wo_ref, o_ref, ub_ref, vt_ref, wqt_ref, wob_ref, s_ref, e_ref, va_ref,
                 *, lam_init, sub, side_heads):
    ctx_len = kc_ref.shape[1]
    seq = kl_ref.shape[1]
    n_sub = q_ref.shape[1] // sub
    lq = dl_ref[...]
    lam = (jnp.exp(jnp.sum(lq[0:1] * lq[1:2], axis=1, keepdims=True))
           - jnp.exp(jnp.sum(lq[2:3] * lq[3:4], axis=1, keepdims=True)) + lam_init)
    nt = (((1,), (1,)), ((), ()))
    lane = lax.broadcasted_iota(jnp.int32, (sub, LANES), 1)
    zero = jnp.zeros((sub, LANES), BF16)

    ones_col = jnp.where(lax.broadcasted_iota(jnp.int32, (ctx_len + seq, LANES), 1) == 0,
                         1.0, 0.0).astype(BF16)
    va_ref[pl.ds(0, ctx_len), pl.ds(0, LANES)] = vc_ref[0]
    va_ref[pl.ds(ctx_len, seq), pl.ds(0, LANES)] = vl_ref[0]
    va_ref[:, pl.ds(LANES, LANES)] = ones_col

    def scores(t):
        q = q_ref[0, pl.ds(t * sub, sub), :]
        q2 = jnp.concatenate([jnp.where(lane < DIFF_QKDIM, q, zero),
                              jnp.where(lane >= DIFF_QKDIM, q, zero)], axis=0)
        s_ref[t % 2, :, pl.ds(0, ctx_len)] = lax.dot_general(
            q2, kc_ref[0], nt, preferred_element_type=F32)
        s_ref[t % 2, :, pl.ds(ctx_len, seq)] = lax.dot_general(
            q2, kl_ref[0], nt, preferred_element_type=F32)

    def numerators(t):
        for r in range(0, 2 * sub, SOFTMAX_ROWS):
            rows = pl.ds(r, SOFTMAX_ROWS)
            s = s_ref[t % 2, rows, :]
            e_ref[t % 2, rows, :] = jnp.exp(s - jnp.max(s, axis=1, keepdims=True)).astype(BF16)

    def outputs(t):
        oa = jnp.dot(e_ref[t % 2], va_ref[...], preferred_element_type=F32)
        o2 = oa[:, :LANES] / oa[:, LANES:LANES + 1]
        o = o2[:sub] - lam * o2[sub:]
        ms = jnp.mean(o * o, axis=1, keepdims=True)
        y = o * lax.rsqrt(ms + EPS) * sg_ref[...]
        o_ref[0, pl.ds(t * sub, sub), :] = (y * (1.0 - lam_init)).astype(BF16)

    for stage in range(n_sub + 2):
        if stage < n_sub:
            scores(stage)
        if 1 <= stage <= n_sub:
            numerators(stage - 1)
        if stage >= 2:
            outputs(stage - 2)

    @pl.when(jnp.logical_and(pl.program_id(1) < side_heads, pl.program_id(2) == 0))
    def _():
        ub_ref[...] = u_ref[...].astype(BF16)
        vt_ref[...] = v_ref[...].T.astype(BF16)

    @pl.when(jnp.logical_and(pl.program_id(1) >= side_heads, pl.program_id(2) == 0))
    def _():
        wqt_ref[...] = wq_ref[...].T.astype(BF16)
        wob_ref[...] = wo_ref[...].astype(BF16)


def _attn_call(z_lat, kv_ctx, diff_lambda, subln_g, peer_u, peer_v, peer_wq, w_out, *, lam_init,
               tq=2048, sub=256):
    bsz, seq, _ = z_lat.shape
    ctx_len = kv_ctx.shape[1]
    tq = min(tq, seq)
    heads = DIFF_HEADS
    qb = POOL_WIDTH // LANES
    kb = qb + heads
    vb = kb + heads
    experts, d = peer_u.shape
    side_heads = 8
    trows = experts // (bsz * side_heads)
    assert trows * bsz * side_heads == experts
    tblk = lambda b, h, i: b * side_heads + jnp.minimum(h, side_heads - 1)
    rest = heads - side_heads
    wrows = peer_wq.shape[0] // (bsz * rest)
    assert wrows * bsz * rest == peer_wq.shape[0] == w_out.shape[0] and wrows % LANES == 0
    wblk = lambda b, h, i: b * rest + jnp.maximum(h - side_heads, 0)
    kern = functools.partial(_attn_kernel, lam_init=lam_init, sub=sub, side_heads=side_heads)
    return pl.pallas_call(
        kern,
        grid=(bsz, heads, seq // tq),
        in_specs=[pl.BlockSpec((1, tq, LANES), lambda b, h, i: (b, i, qb + h)),
                  pl.BlockSpec((1, ctx_len, LANES), lambda b, h, i: (b, 0, h)),
                  pl.BlockSpec((1, ctx_len, LANES), lambda b, h, i: (b, 0, heads + h)),
                  pl.BlockSpec((1, seq, LANES), lambda b, h, i: (b, 0, kb + h)),
                  pl.BlockSpec((1, seq, LANES), lambda b, h, i: (b, 0, vb + h)),
                  pl.BlockSpec(diff_lambda.shape, lambda b, h, i: (0, 0)),
                  pl.BlockSpec((1, LANES), lambda b, h, i: (0, 0)),
                  pl.BlockSpec((trows, d), lambda b, h, i: (tblk(b, h, i), 0)),
                  pl.BlockSpec((trows, d), lambda b, h, i: (tblk(b, h, i), 0)),
                  pl.BlockSpec((wrows, peer_wq.shape[1]), lambda b, h, i: (wblk(b, h, i), 0)),
                  pl.BlockSpec((wrows, w_out.shape[1]), lambda b, h, i: (wblk(b, h, i), 0))],
        out_specs=[pl.BlockSpec((1, tq, LANES), lambda b, h, i: (b, i, h)),
                   pl.BlockSpec((trows, d), lambda b, h, i: (tblk(b, h, i), 0)),
                   pl.BlockSpec((d, trows), lambda b, h, i: (0, tblk(b, h, i))),
                   pl.BlockSpec((peer_wq.shape[1], wrows), lambda b, h, i: (0, wblk(b, h, i))),
                   pl.BlockSpec((wrows, w_out.shape[1]), lambda b, h, i: (wblk(b, h, i), 0))],
        out_shape=[jax.ShapeDtypeStruct((bsz, seq, heads * DIFF_VDIM), BF16),
                   jax.ShapeDtypeStruct((experts, d), BF16),
                   jax.ShapeDtypeStruct((d, experts), BF16),
                   jax.ShapeDtypeStruct(peer_wq.shape[::-1], BF16),
                   jax.ShapeDtypeStruct(w_out.shape, BF16)],
        scratch_shapes=[pltpu.VMEM((2, 2 * sub, ctx_len + seq), F32),
                        pltpu.VMEM((2, 2 * sub, ctx_len + seq), BF16),
                        pltpu.VMEM((ctx_len + seq, 2 * LANES), BF16)],
        compiler_params=_cparams(("parallel", "arbitrary", "arbitrary")),
        name="diff_attn",
    )(z_lat, kv_ctx, kv_ctx, z_lat, z_lat, diff_lambda, subln_g, peer_u, peer_v, peer_wq, w_out)


def _pool_kernel(z_ref, w_ref, b_ref, s_ref, o_ref, zp_ref):
    seq = z_ref.shape[1]
    zp_ref[...] = jnp.zeros_like(zp_ref)
    zp_ref[pl.ds(POOL_PAD, seq), :] = z_ref[0].astype(F32)
    t = lax.broadcasted_iota(jnp.int32, (seq, POOL_GROUP_DIM), 0)
    outs = []
    for g, win in enumerate(POOL_WINDOWS):
        half = win // 2
        cols = pl.ds(g * POOL_GROUP_DIM, POOL_GROUP_DIM)
        acc = jnp.zeros((seq, POOL_GROUP_DIM), F32)
        for k in range(-half, half):
            acc = acc + zp_ref[pl.ds(POOL_PAD + k, seq), cols]
        cnt = (jnp.minimum(t + half, seq) - jnp.maximum(t - half, 0)).astype(F32)
        y = acc / cnt - zp_ref[pl.ds(POOL_PAD, seq), cols]
        r = jnp.dot(y.astype(BF16), w_ref[g].astype(BF16), preferred_element_type=F32)
        outs.append((r + b_ref[g]) * s_ref[g])
    o_ref[0] = jnp.concatenate(outs, axis=1).astype(BF16)


def _pool_call(z_lat, pool_w, pool_b, pool_scale):
    bsz, seq, _ = z_lat.shape
    ng = len(POOL_WINDOWS)
    return pl.pallas_call(
        _pool_kernel,
        grid=(bsz,),
        in_specs=[pl.BlockSpec((1, seq, POOL_WIDTH), lambda b: (b, 0, 0)),
                  pl.BlockSpec((ng, POOL_GROUP_DIM, POOL_GROUP_DIM), lambda b: (0, 0, 0)),
                  pl.BlockSpec((ng, 1, POOL_GROUP_DIM), lambda b: (0, 0, 0)),
                  pl.BlockSpec((ng, 1, POOL_GROUP_DIM), lambda b: (0, 0, 0))],
        out_specs=pl.BlockSpec((1, seq, POOL_WIDTH), lambda b: (b, 0, 0)),
        out_shape=jax.ShapeDtypeStruct((bsz, seq, POOL_WIDTH), BF16),
        scratch_shapes=[pltpu.VMEM((seq + 2 * POOL_PAD, POOL_WIDTH), F32)],
        compiler_params=_cparams(("parallel",)),
        name="pool_mix",
    )(z_lat, pool_w, pool_b.reshape(ng, 1, POOL_GROUP_DIM),
      pool_scale.reshape(ng, 1, POOL_GROUP_DIM))


def _outproj_kernel(x_ref, p_ref, a_ref, w_ref, g_ref, o_ref):
    r = (jnp.dot(p_ref[0], w_ref[pl.ds(0, POOL_WIDTH), :], preferred_element_type=F32)
         + jnp.dot(a_ref[0], w_ref[pl.ds(POOL_WIDTH, a_ref.shape[2]), :],
                   preferred_element_type=F32))
    o_ref[0] = x_ref[0] + g_ref[0] * r


def _outproj_call(x, pool_y, attn, w_out, mod3, tm=256):
    bsz, seq, d = x.shape
    aw = attn.shape[2]
    return pl.pallas_call(
        _outproj_kernel,
        grid=(bsz, seq // tm),
        in_specs=[pl.BlockSpec((1, tm, d), lambda b, i: (b, i, 0)),
                  pl.BlockSpec((1, tm, POOL_WIDTH), lambda b, i: (b, i, 0)),
                  pl.BlockSpec((1, tm, aw), lambda b, i: (b, i, 0)),
                  pl.BlockSpec(w_out.shape, lambda b, i: (0, 0)),
                  pl.BlockSpec((1, 1, d), lambda b, i: (b, 0, 2))],
        out_specs=pl.BlockSpec((1, tm, d), lambda b, i: (b, i, 0)),
        out_shape=jax.ShapeDtypeStruct((bsz, seq, d), F32),
        compiler_params=_cparams(("parallel", "parallel")),
        name="outproj_residual",
    )(x, pool_y, attn, w_out, mod3)


def _top16_rows(s):
    n, t = s.shape
    row = lax.broadcasted_iota(jnp.int32, (n, t), 0).astype(F32)
    row16 = lax.broadcasted_iota(jnp.int32, (PEER_TOPK, t), 0)
    work = s
    rank = jnp.full((n, t), 127.0, F32)
    vals = jnp.zeros((PEER_TOPK, t), F32)
    for i in range(PEER_TOPK):
        m = jnp.max(work, axis=0, keepdims=True)
        idx = jnp.min(jnp.where(work == m, row, float(n)), axis=0, keepdims=True)
        sel = row == idx
        rank = jnp.where(sel, float(i), rank)
        vals = jnp.where(row16 == i, m, vals)
        work = jnp.where(sel, NEG_INF, work)
    return vals, rank


_CAND_SLABS = ((0, 0, 8), (0, 8, 8), (1, 0, 8), (2, 0, 5), (3, 0, 4), (4, 0, 3),
               (5, 0, 2), (6, 0, 2), (7, 0, 2))


def _pair_select(va, vb):
    t = va.shape[1]
    sub = lax.broadcasted_iota(jnp.int32, (8, t), 0)
    subf = sub.astype(F32)
    cands, flats = [], []
    for (i, j0, lim) in _CAND_SLABS:
        c = va[i:i + 1, :] + vb[j0:j0 + 8, :]
        cands.append(jnp.where(sub < lim, c, NEG_INF))
        flats.append(subf + float(i * PEER_TOPK + j0))
    cands.append(va[8:16, :] + vb[0:1, :])
    flats.append(subf * float(PEER_TOPK) + float(8 * PEER_TOPK))
    cand = jnp.concatenate(cands, axis=0)
    flat = jnp.concatenate(flats, axis=0)
    work = cand
    big = float(PEER_TOPK * PEER_TOPK)
    for _ in range(PEER_TOPK):
        m = jnp.max(work, axis=0, keepdims=True)
        idx = jnp.min(jnp.where(work == m, flat, big), axis=0, keepdims=True)
        work = jnp.where(flat == idx, NEG_INF, work)
    sel = jnp.logical_and(work == NEG_INF, cand > NEG_INF)
    self32 = sel.astype(F32)
    top = va[0:1, :] + vb[0:1, :]
    z = jnp.sum(jnp.where(sel, jnp.exp(cand - top), 0.0), axis=0, keepdims=True)
    counts = [jnp.sum(self32[0:16], axis=0, keepdims=True)]
    for k in range(2, 9):
        counts.append(jnp.sum(self32[8 * k:8 * k + 8], axis=0, keepdims=True))
    n = jnp.concatenate(counts + [self32[72:80]], axis=0)
    return n, z


def _sort_network(n):
    size = 16
    pairs = []
    p = 1
    while p < size:
        k = p
        while k >= 1:
            for j in range(k % p, size - k, 2 * k):
                for i in range(min(k, size - j - k)):
                    if (i + j) // (2 * p) == (i + j + k) // (2 * p):
                        pairs.append((i + j, i + j + k))
            k //= 2
        p *= 2
    return [(i, j) for (i, j) in pairs if j < n]


def _sorted_top16(slabs):
    s = list(slabs)
    depth = len(s)
    for i, j in _sort_network(depth):
        s[i], s[j] = jnp.maximum(s[i], s[j]), jnp.minimum(s[i], s[j])
    t = s[0].shape[1]
    sub = lax.broadcasted_iota(jnp.int32, (8, t), 0).astype(F32)
    rows = []
    for i in range(PEER_TOPK):
        m = jnp.max(s[0], axis=0, keepdims=True)
        rows.append(m)
        if i == PEER_TOPK - 1:
            break
        first = jnp.min(jnp.where(s[0] == m, sub, 8.0), axis=0, keepdims=True)
        win = sub == first
        for r in range(min(depth, PEER_TOPK - 1 - i)):
            s[r] = jnp.where(win, s[r + 1] if r + 1 < depth else NEG_INF, s[r])
    return rows


def _tie_rows(s, v):
    cnt = jnp.sum(jnp.where(s >= v[PEER_TOPK - 1], 1.0, 0.0), axis=0, keepdims=True)
    tie = jnp.where(cnt != float(PEER_TOPK), 1.0, 0.0)
    for i in range(PEER_TOPK - 1):
        tie = tie + jnp.where(v[i] == v[i + 1], 1.0, 0.0)
    return tie


def _pair_select_fast(va, vb):
    t = va[0].shape[1]
    row16 = lax.broadcasted_iota(jnp.int32, (PEER_TOPK, t), 0)
    sub = lax.broadcasted_iota(jnp.int32, (8, t), 0)
    va_arr = jnp.zeros((PEER_TOPK, t), F32)
    vb_arr = jnp.zeros((PEER_TOPK, t), F32)
    for i in range(PEER_TOPK):
        va_arr = jnp.where(row16 == i, va[i], va_arr)
        vb_arr = jnp.where(row16 == i, vb[i], vb_arr)
    cands = []
    for (i, j0, lim) in _CAND_SLABS:
        cands.append(jnp.where(sub < lim, va[i] + vb_arr[j0:j0 + 8, :], NEG_INF))
    cands.append(va_arr[8:16, :] + vb[0])
    tau = _sorted_top16(cands)[PEER_TOPK - 1]
    top = va[0] + vb[0]
    sel = [jnp.where(c >= tau, 1.0, 0.0) for c in cands]
    z = sum(jnp.sum(s * jnp.exp(c - top), axis=0, keepdims=True) for s, c in zip(sel, cands))
    n = [jnp.sum(sel[0] + sel[1], axis=0, keepdims=True)]
    for k in range(2, 9):
        n.append(jnp.sum(sel[k], axis=0, keepdims=True))
    for r in range(8):
        n.append(sel[9][r:r + 1, :])
    total = sum(n[:8]) + jnp.sum(sel[9], axis=0, keepdims=True)
    return n, z, jnp.where(total != float(PEER_TOPK), 1.0, 0.0)


def _peer_sel_kernel(x_ref, g_ref, sh_ref, sc_ref, wq_ref, keys_ref,
                     f_ref, na_ref, ea_ref, rb_ref, eb_ref, qt_ref):
    ft = _norm_mod(x_ref[0], g_ref[...], sh_ref[0], sc_ref[0]).T.astype(BF16)
    f_ref[...] = ft
    qt_ref[...] = jnp.dot(wq_ref[...], ft, preferred_element_type=F32)

    def scores(hp):
        k_hi, k_lo = _split_bf16(keys_ref[hp])
        q_hi, q_lo = _split_bf16(qt_ref[pl.ds(pl.multiple_of(hp * PEER_HALF, PEER_HALF),
                                              PEER_HALF), :])
        return (jnp.dot(k_hi, q_hi, preferred_element_type=F32)
                + jnp.dot(k_hi, q_lo, preferred_element_type=F32)
                + jnp.dot(k_lo, q_hi, preferred_element_type=F32))

    def head(h, carry):
        sa = scores(2 * h)
        sb = scores(2 * h + 1)

        va = _sorted_top16([sa[8 * k:8 * k + 8] for k in range(PEER_NKEYS // 8)])
        vb = _sorted_top16([sb[8 * k:8 * k + 8] for k in range(PEER_NKEYS // 8)])
        n_rows, z, tie2 = _pair_select_fast(va, vb)
        na = jnp.zeros_like(sa)
        rank_b = jnp.zeros_like(sb)
        for i in range(PEER_TOPK):
            na = jnp.where(sa == va[i], n_rows[i], na)
            rank_b = jnp.where(vb[i] > sb, float(i + 1), rank_b)
        na_ref[h] = na
        ea_ref[h] = jnp.exp(sa - va[0]) / z
        rb_ref[h] = rank_b.astype(BF16)
        eb_ref[h] = jnp.exp(sb - vb[0]).astype(BF16)

        tie = tie2 + _tie_rows(sa, va) + _tie_rows(sb, vb)

        @pl.when(jnp.max(tie) > 0.0)
        def _():
            va_t, rank_a_t = _top16_rows(sa)
            vb_t, rank_b_t = _top16_rows(sb)
            n_t, z_t = _pair_select(va_t, vb_t)
            na_t = jnp.zeros_like(sa)
            for i in range(PEER_TOPK):
                na_t = jnp.where(rank_a_t == float(i), n_t[i:i + 1, :], na_t)
            na_ref[h] = na_t
            ea_ref[h] = jnp.exp(sa - va_t[0:1, :]) / z_t
            rb_ref[h] = rank_b_t.astype(BF16)
            eb_ref[h] = jnp.exp(sb - vb_t[0:1, :]).astype(BF16)

        return carry

    lax.fori_loop(0, PEER_HEADS, head, 0)


def _peer_sel_call(x1, norm_g, mod3, wq_t, keys, ts=512):
    bsz, seq, d = x1.shape
    tokens = bsz * seq
    nblk = seq // ts
    sel_f32 = jax.ShapeDtypeStruct((PEER_HEADS, PEER_NKEYS, tokens), F32)
    sel_bf16 = jax.ShapeDtypeStruct((PEER_HEADS, PEER_NKEYS, tokens), BF16)
    sel_spec = pl.BlockSpec((PEER_HEADS, PEER_NKEYS, ts), lambda b, i: (0, 0, b * nblk + i))
    return pl.pallas_call(
        _peer_sel_kernel,
        grid=(bsz, nblk),
        in_specs=[pl.BlockSpec((1, ts, d), lambda b, i: (b, i, 0)),
                  pl.BlockSpec((1, d), lambda b, i: (0, 0)),
                  pl.BlockSpec((1, 1, d), lambda b, i: (b, 0, 3)),
                  pl.BlockSpec((1, 1, d), lambda b, i: (b, 0, 4)),
                  pl.BlockSpec(wq_t.shape, lambda b, i: (0, 0)),
                  pl.BlockSpec(keys.shape, lambda b, i: (0, 0, 0))],
        out_specs=[pl.BlockSpec((d, ts), lambda b, i: (0, b * nblk + i)),
                   sel_spec, sel_spec, sel_spec, sel_spec],
        out_shape=[jax.ShapeDtypeStruct((d, tokens), BF16),
                   sel_f32, sel_f32, sel_bf16, sel_bf16],
        scratch_shapes=[pltpu.VMEM((wq_t.shape[0], ts), F32)],
        compiler_params=_cparams(("parallel", "parallel")),
        name="peer_select",
    )(x1, norm_g, mod3, mod3, wq_t, keys)


def _gelu(x):
    return 0.5 * x * (1.0 + lax.erf(x * (1.0 / math.sqrt(2.0))))


def _peer_kernel(f_ref, u_ref, vt_ref, na_ref, ea_ref, rb_ref, eb_ref,
                 x_ref, g2_ref, fg_ref, o_ref, act_ref, p_ref, acc_ref):
    e = pl.program_id(2)
    last = pl.num_programs(2) - 1
    tm = f_ref.shape[1]
    sub = u_ref.shape[0] // 2
    keys_per_sub = sub // PEER_NKEYS
    zero = jnp.zeros((PEER_NKEYS, BF16_TILE_LANES), BF16)

    def activations(j):
        act_ref[j] = jnp.dot(u_ref[pl.ds(j * sub, sub), :], f_ref[...],
                             preferred_element_type=F32)

    def gated(j, n_ref, g_ref, slot):
        for lc in range(tm // BF16_TILE_LANES):
            cols = pl.ds(lc * BF16_TILE_LANES, BF16_TILE_LANES)
            for al in range(keys_per_sub):
                a = j * keys_per_sub + al
                w = zero
                for h in range(PEER_HEADS):
                    na = n_ref[h, pl.ds(a, 1), cols].astype(BF16)
                    ea = g_ref[h, pl.ds(a, 1), cols].astype(BF16)
                    w = w + jnp.where(rb_ref[h, :, cols] < na, eb_ref[h, :, cols], zero) * ea
                rows = pl.ds(al * PEER_NKEYS, PEER_NKEYS)
                g = _gelu(act_ref[j, rows, cols]).astype(BF16)
                p_ref[slot, pl.ds(j * sub + al * PEER_NKEYS, PEER_NKEYS), cols] = g * w

    def values(slot):
        acc_ref[...] += jnp.dot(vt_ref[...], p_ref[slot], preferred_element_type=F32)

    this = e % 2
    before = 1 - this

    @pl.when(e == 0)
    def _():
        acc_ref[...] = jnp.zeros_like(acc_ref)
        activations(0)
        activations(1)
        gated(0, na_ref, ea_ref, this)
        gated(1, na_ref, ea_ref, this)

    @pl.when(jnp.logical_and(e > 0, e < last))
    def _():
        activations(0)
        activations(1)
        values(before)
        gated(0, na_ref, ea_ref, this)
        gated(1, na_ref, ea_ref, this)

    @pl.when(e == last)
    def _():
        values(before)
        y = x_ref[0] + g2_ref[0] * acc_ref[...].T
        ms = jnp.mean(y * y, axis=1, keepdims=True)
        o_ref[0] = y * lax.rsqrt(ms + EPS) * fg_ref[...]


def _peer_call(f, u, v_t, na, ea, rb, eb, x1, mod3, final_g, tm=512, ec=1024):
    bsz, seq, d = x1.shape
    experts = u.shape[0]
    nblk = seq // tm
    nchunk = experts // ec
    a_per_step = ec // PEER_NKEYS
    tok = lambda b, i, e: b * nblk + i
    cur = lambda e: jnp.minimum(e, nchunk - 1)
    prev = lambda e: jnp.maximum(e - 1, 0)
    sel_rows = lambda which: pl.BlockSpec(
        (PEER_HEADS, a_per_step, tm), lambda b, i, e: (0, which(e), tok(b, i, e)))
    sel_all = pl.BlockSpec((PEER_HEADS, PEER_NKEYS, tm), lambda b, i, e: (0, 0, tok(b, i, e)))
    return pl.pallas_call(
        _peer_kernel,
        grid=(bsz, nblk, nchunk + 1),
        in_specs=[pl.BlockSpec((d, tm), lambda b, i, e: (0, tok(b, i, e))),
                  pl.BlockSpec((ec, d), lambda b, i, e: (cur(e), 0)),
                  pl.BlockSpec((d, ec), lambda b, i, e: (0, prev(e))),
                  sel_rows(cur), sel_rows(cur), sel_all, sel_all,
                  pl.BlockSpec((1, tm, d), lambda b, i, e: (b, i, 0)),
                  pl.BlockSpec((1, 1, d), lambda b, i, e: (b, 0, 5)),
                  pl.BlockSpec((1, d), lambda b, i, e: (0, 0))],
        out_specs=pl.BlockSpec((1, tm, d), lambda b, i, e: (b, i, 0)),
        out_shape=jax.ShapeDtypeStruct((bsz, seq, d), F32),
        scratch_shapes=[pltpu.VMEM((2, ec // 2, tm), F32),
                        pltpu.VMEM((2, ec, tm), BF16),
                        pltpu.VMEM((d, tm), F32)],
        compiler_params=_cparams(("parallel", "parallel", "arbitrary")),
        name="peer_dense",
    )(f, u, v_t, na, ea, rb, eb, x1, mod3, final_g)


def _rope_tables(seq):
    pos = np.arange(seq)
    row = (pos // GRID_W).astype(np.float32)
    col = (pos % GRID_W).astype(np.float32)
    inv_freq = (ROPE_BASE ** (-np.arange(ROPE_HALF, dtype=np.float32) / ROPE_HALF)).astype(np.float32)
    lane = np.arange(LANES)
    axis = (lane % DIFF_QKDIM) // ROPE_AXIS_DIM
    freq = inv_freq[lane % ROPE_HALF]
    p = np.where(axis[None, :] == 0, row[:, None], col[:, None]).astype(np.float32)
    ang = p * freq[None, :]
    sign = np.where((lane % ROPE_AXIS_DIM) < ROPE_HALF, -1.0, 1.0).astype(np.float32)
    return jnp.asarray(np.cos(ang), F32), jnp.asarray(np.sin(ang) * sign[None, :], F32)


def kernel(x, c, ctx, c_ctx, ada_w, ada_b, norm1_g, w_in, pool_w, pool_b, pool_scale,
           diff_lambda, subln_g, w_out, norm2_g, peer_wq, peer_keys, peer_u, peer_v, final_g):
    bsz, seq, d = x.shape
    ctx_len = ctx.shape[1]
    layer = 0
    lam_init = 0.8 - 0.6 * math.exp(-0.3 * layer)
    mod_rows = 8
    assert bsz + 1 <= mod_rows

    cc = jnp.concatenate([c, c_ctx[None, :], jnp.zeros((mod_rows - bsz - 1, d), F32)], axis=0)
    mod = _ada_call(cc, ada_w[layer], ada_b[layer][None, :])
    mod3 = mod.reshape(mod_rows, 1, 6 * d)

    w_in_b = w_in[layer].astype(BF16)
    g1 = norm1_g[layer][None, :]
    cos, sin = _rope_tables(seq)
    in_width = w_in_b.shape[1]
    attn_width = (in_width - POOL_WIDTH) // 3
    z_lat = _inproj_call(x, g1, mod3, lambda b: b, w_in_b, cos, sin,
                         rope=True, col0=0, ncols=in_width)
    kv_ctx = _inproj_call(ctx, g1, mod3, lambda b: bsz, w_in_b, cos, sin,
                          rope=False, col0=POOL_WIDTH + attn_width, ncols=2 * attn_width)

    attn, u_b, v_t, wq_t, w_out_b = _attn_call(
        z_lat, kv_ctx, diff_lambda[layer], subln_g[layer][None, :],
        peer_u[layer], peer_v[layer], peer_wq[layer], w_out[layer], lam_init=lam_init)
    pool_y = _pool_call(z_lat, pool_w[layer], pool_b[layer], pool_scale[layer])
    x1 = _outproj_call(x, pool_y, attn, w_out_b, mod3)

    keys = peer_keys[layer].reshape(2 * PEER_HEADS, PEER_NKEYS, PEER_HALF)
    f, na, ea, rb, eb = _peer_sel_call(x1, norm2_g[layer][None, :], mod3, wq_t, keys)
    return _peer_call(f, u_b, v_t, na, ea, rb, eb, x1, mod3, final_g[None, :])
```

```python
import functools
import math

import jax
import jax.numpy as jnp
import numpy as np
from jax import lax
from jax.experimental import pallas as pl
from jax.experimental.pallas import tpu as pltpu

F32 = jnp.float32
BF16 = jnp.bfloat16

EPS = 1e-6
GRID_W = 64
POOL_WINDOWS = (2, 4, 8, 16)
POOL_GROUP_DIM = 128
POOL_WIDTH = POOL_GROUP_DIM * len(POOL_WINDOWS)
DIFF_VDIM = 128
DIFF_HEADS = 12
DIFF_QKDIM = 64
ROPE_BASE = 10000.0
ROPE_AXIS_DIM = 32
ROPE_HALF = ROPE_AXIS_DIM // 2
PEER_HEADS = 8
PEER_NKEYS = 128
PEER_HALF = 128
PEER_TOPK = 16
LANES = 128
BF16_TILE_LANES = 256
SOFTMAX_ROWS = 16
POOL_PAD = 8
VMEM_LIMIT_BYTES = 56 * 1024 * 1024
NEG_INF = float("-inf")


def _cparams(sem):
    return pltpu.CompilerParams(dimension_semantics=sem, vmem_limit_bytes=VMEM_LIMIT_BYTES)


def _split_bf16(a):
    hi = a.astype(BF16)
    lo = (a - hi.astype(F32)).astype(BF16)
    return hi, lo


def _ada_kernel(c_ref, w_ref, b_ref, o_ref):
    c = c_ref[...]
    s = c * (1.0 / (1.0 + jnp.exp(-c)))
    o_ref[...] = jnp.dot(s.astype(BF16), w_ref[...].astype(BF16),
                         preferred_element_type=F32) + b_ref[...]


def _ada_call(cc, w, b, tn=2048):
    rows, d = cc.shape
    n = w.shape[1]
    return pl.pallas_call(
        _ada_kernel,
        grid=(n // tn,),
        in_specs=[pl.BlockSpec((rows, d), lambda j: (0, 0)),
                  pl.BlockSpec((d, tn), lambda j: (0, j)),
                  pl.BlockSpec((1, tn), lambda j: (0, j))],
        out_specs=pl.BlockSpec((rows, tn), lambda j: (0, j)),
        out_shape=jax.ShapeDtypeStruct((rows, n), F32),
        compiler_params=_cparams(("arbitrary",)),
        name="ada_mod",
    )(cc, w, b)


def _norm_mod(xf, g, sh, sc):
    ms = jnp.mean(xf * xf, axis=-1, keepdims=True)
    y = xf * lax.rsqrt(ms + EPS) * g
    return y * (1.0 + sc) + sh


def _inproj_kernel(x_ref, g_ref, sh_ref, sc_ref, w_ref, cos_ref, sin_ref, o_ref, h_ref, z_ref,
                   *, rope, col0, tn):
    n_tiles = o_ref.shape[2] // tn
    q_tiles = (DIFF_VDIM * DIFF_HEADS) // tn
    pool_tiles = POOL_WIDTH // tn

    def matmul(n):
        z_ref[n % 2] = jnp.dot(h_ref[...], w_ref[:, pl.ds(col0 + n * tn, tn)],
                               preferred_element_type=F32)

    def epilogue(n):
        z = z_ref[n % 2]
        if rope and pool_tiles <= n < pool_tiles + 2 * q_tiles:
            reps = tn // LANES
            cos = jnp.concatenate([cos_ref[...]] * reps, axis=1)
            sin = jnp.concatenate([sin_ref[...]] * reps, axis=1)
            lane = lax.broadcasted_iota(jnp.int32, z.shape, 1)
            first = (lane % ROPE_AXIS_DIM) < ROPE_HALF
            partner = jnp.where(first, pltpu.roll(z, tn - ROPE_HALF, 1),
                                pltpu.roll(z, ROPE_HALF, 1))
            z = z * cos + partner * sin
            if n < pool_tiles + q_tiles:
                z = z * DIFF_QKDIM ** -0.5
        o_ref[0, :, pl.ds(n * tn, tn)] = z.astype(BF16)

    h = _norm_mod(x_ref[0], g_ref[...], sh_ref[0], sc_ref[0])
    h_ref[...] = h.astype(BF16)
    for stage in range(n_tiles + 1):
        if stage < n_tiles:
            matmul(stage)
        if stage >= 1:
            epilogue(stage - 1)


def _inproj_call(x, g, mod3, mod_row_fn, w, cos, sin, *, rope, col0, ncols, tm=512, tn=512):
    bsz, rows, d = x.shape
    tm = min(tm, rows)
    kern = functools.partial(_inproj_kernel, rope=rope, col0=col0, tn=tn)
    return pl.pallas_call(
        kern,
        grid=(bsz, rows // tm),
        in_specs=[pl.BlockSpec((1, tm, d), lambda b, i: (b, i, 0)),
                  pl.BlockSpec((1, d), lambda b, i: (0, 0)),
                  pl.BlockSpec((1, 1, d), lambda b, i: (mod_row_fn(b), 0, 0)),
                  pl.BlockSpec((1, 1, d), lambda b, i: (mod_row_fn(b), 0, 1)),
                  pl.BlockSpec(w.shape, lambda b, i: (0, 0), pipeline_mode=pl.Buffered(1)),
                  pl.BlockSpec((tm, LANES), lambda b, i: (i, 0)),
                  pl.BlockSpec((tm, LANES), lambda b, i: (i, 0))],
        out_specs=pl.BlockSpec((1, tm, ncols), lambda b, i: (b, i, 0)),
        out_shape=jax.ShapeDtypeStruct((bsz, rows, ncols), BF16),
        scratch_shapes=[pltpu.VMEM((tm, d), BF16),
                        pltpu.VMEM((2, tm, tn), F32)],
        compiler_params=_cparams(("parallel", "arbitrary")),
        name="inproj_rope" if rope else "inproj_ctx",
    )(x, g, mod3, mod3, w, cos, sin)


def _attn_kernel(q_ref, kc_ref, vc_ref, kl_ref, vl_ref, dl_ref, sg_ref, u_ref, v_ref, wq_ref,
                 wo_ref, o_ref, ub_ref, vt_ref, wqt_ref, wob_ref, s_ref, e_ref, va_ref,
                 *, lam_init, sub):
    ctx_len = kc_ref.shape[1]
    seq = kl_ref.shape[1]
    n_sub = q_ref.shape[1] // sub
    lq = dl_ref[...]
    lam = (jnp.exp(jnp.sum(lq[0:1] * lq[1:2], axis=1, keepdims=True))
           - jnp.exp(jnp.sum(lq[2:3] * lq[3:4], axis=1, keepdims=True)) + lam_init)
    nt = (((1,), (1,)), ((), ()))
    lane = lax.broadcasted_iota(jnp.int32, (sub, LANES), 1)
    zero = jnp.zeros((sub, LANES), BF16)

    ones_col = jnp.where(lax.broadcasted_iota(jnp.int32, (ctx_len + seq, LANES), 1) == 0,
                         1.0, 0.0).astype(BF16)
    va_ref[pl.ds(0, ctx_len), pl.ds(0, LANES)] = vc_ref[0]
    va_ref[pl.ds(ctx_len, seq), pl.ds(0, LANES)] = vl_ref[0]
    va_ref[:, pl.ds(LANES, LANES)] = ones_col

    def scores(t):
        q = q_ref[0, pl.ds(t * sub, sub), :]
        q2 = jnp.concatenate([jnp.where(lane < DIFF_QKDIM, q, zero),
                              jnp.where(lane >= DIFF_QKDIM, q, zero)], axis=0)
        s_ref[t % 2, :, pl.ds(0, ctx_len)] = lax.dot_general(
            q2, kc_ref[0], nt, preferred_element_type=F32)
        s_ref[t % 2, :, pl.ds(ctx_len, seq)] = lax.dot_general(
            q2, kl_ref[0], nt, preferred_element_type=F32)

    def numerators(t):
        for r in range(0, 2 * sub, SOFTMAX_ROWS):
            rows = pl.ds(r, SOFTMAX_ROWS)
            s = s_ref[t % 2, rows, :]
            e_ref[t % 2, rows, :] = jnp.exp(s - jnp.max(s, axis=1, keepdims=True)).astype(BF16)

    def outputs(t):
        oa = jnp.dot(e_ref[t % 2], va_ref[...], preferred_element_type=F32)
        o2 = oa[:, :LANES] / oa[:, LANES:LANES + 1]
        o = o2[:sub] - lam * o2[sub:]
        ms = jnp.mean(o * o, axis=1, keepdims=True)
        y = o * lax.rsqrt(ms + EPS) * sg_ref[...]
        o_ref[0, pl.ds(t * sub, sub), :] = (y * (1.0 - lam_init)).astype(BF16)

    for stage in range(n_sub + 2):
        if stage < n_sub:
            scores(stage)
        if 1 <= stage <= n_sub:
            numerators(stage - 1)
        if stage >= 2:
            outputs(stage - 2)

    ub_ref[...] = u_ref[...].astype(BF16)
    vt_ref[...] = v_ref[...].T.astype(BF16)
    wqt_ref[...] = wq_ref[...].T.astype(BF16)
    wob_ref[...] = wo_ref[...].astype(BF16)


def _attn_call(z_lat, kv_ctx, diff_lambda, subln_g, peer_u, peer_v, peer_wq, w_out, *, lam_init,
               tq=2048, sub=256):
    bsz, seq, _ = z_lat.shape
    ctx_len = kv_ctx.shape[1]
    tq = min(tq, seq)
    heads = DIFF_HEADS
    qb = POOL_WIDTH // LANES
    kb = qb + heads
    vb = kb + heads
    experts, d = peer_u.shape
    side_heads = 8
    trows = experts // (bsz * side_heads)
    assert trows * bsz * side_heads == experts
    tblk = lambda b, h, i: b * side_heads + jnp.minimum(h, side_heads - 1)
    rest = heads - side_heads
    wrows = peer_wq.shape[0] // (bsz * rest)
    assert wrows * bsz * rest == peer_wq.shape[0] == w_out.shape[0] and wrows % LANES == 0
    wblk = lambda b, h, i: b * rest + jnp.maximum(h - side_heads, 0)
    kern = functools.partial(_attn_kernel, lam_init=lam_init, sub=sub)
    return pl.pallas_call(
        kern,
        grid=(bsz, heads, seq // tq),
        in_specs=[pl.BlockSpec((1, tq, LANES), lambda b, h, i: (b, i, qb + h)),
                  pl.BlockSpec((1, ctx_len, LANES), lambda b, h, i: (b, 0, h)),
                  pl.BlockSpec((1, ctx_len, LANES), lambda b, h, i: (b, 0, heads + h)),
                  pl.BlockSpec((1, seq, LANES), lambda b, h, i: (b, 0, kb + h)),
                  pl.BlockSpec((1, seq, LANES), lambda b, h, i: (b, 0, vb + h)),
                  pl.BlockSpec(diff_lambda.shape, lambda b, h, i: (0, 0)),
                  pl.BlockSpec((1, LANES), lambda b, h, i: (0, 0)),
                  pl.BlockSpec((trows, d), lambda b, h, i: (tblk(b, h, i), 0)),
                  pl.BlockSpec((trows, d), lambda b, h, i: (tblk(b, h, i), 0)),
                  pl.BlockSpec((wrows, peer_wq.shape[1]), lambda b, h, i: (wblk(b, h, i), 0)),
                  pl.BlockSpec((wrows, w_out.shape[1]), lambda b, h, i: (wblk(b, h, i), 0))],
        out_specs=[pl.BlockSpec((1, tq, LANES), lambda b, h, i: (b, i, h)),
                   pl.BlockSpec((trows, d), lambda b, h, i: (tblk(b, h, i), 0)),
                   pl.BlockSpec((d, trows), lambda b, h, i: (0, tblk(b, h, i))),
                   pl.BlockSpec((peer_wq.shape[1], wrows), lambda b, h, i: (0, wblk(b, h, i))),
                   pl.BlockSpec((wrows, w_out.shape[1]), lambda b, h, i: (wblk(b, h, i), 0))],
        out_shape=[jax.ShapeDtypeStruct((bsz, seq, heads * DIFF_VDIM), BF16),
                   jax.ShapeDtypeStruct((experts, d), BF16),
                   jax.ShapeDtypeStruct((d, experts), BF16),
                   jax.ShapeDtypeStruct(peer_wq.shape[::-1], BF16),
                   jax.ShapeDtypeStruct(w_out.shape, BF16)],
        scratch_shapes=[pltpu.VMEM((2, 2 * sub, ctx_len + seq), F32),
                        pltpu.VMEM((2, 2 * sub, ctx_len + seq), BF16),
                        pltpu.VMEM((ctx_len + seq, 2 * LANES), BF16)],
        compiler_params=_cparams(("parallel", "arbitrary", "arbitrary")),
        name="diff_attn",
    )(z_lat, kv_ctx, kv_ctx, z_lat, z_lat, diff_lambda, subln_g, peer_u, peer_v, peer_wq, w_out)


def _pool_kernel(z_ref, w_ref, b_ref, s_ref, o_ref, zp_ref):
    seq = z_ref.shape[1]
    zp_ref[...] = jnp.zeros_like(zp_ref)
    zp_ref[pl.ds(POOL_PAD, seq), :] = z_ref[0].astype(F32)
    t = lax.broadcasted_iota(jnp.int32, (seq, POOL_GROUP_DIM), 0)
    outs = []
    for g, win in enumerate(POOL_WINDOWS):
        half = win // 2
        cols = pl.ds(g * POOL_GROUP_DIM, POOL_GROUP_DIM)
        acc = jnp.zeros((seq, POOL_GROUP_DIM), F32)
        for k in range(-half, half):
            acc = acc + zp_ref[pl.ds(POOL_PAD + k, seq), cols]
        cnt = (jnp.minimum(t + half, seq) - jnp.maximum(t - half, 0)).astype(F32)
        y = acc / cnt - zp_ref[pl.ds(POOL_PAD, seq), cols]
        r = jnp.dot(y.astype(BF16), w_ref[g].astype(BF16), preferred_element_type=F32)
        outs.append((r + b_ref[g]) * s_ref[g])
    o_ref[0] = jnp.concatenate(outs, axis=1).astype(BF16)


def _pool_call(z_lat, pool_w, pool_b, pool_scale):
    bsz, seq, _ = z_lat.shape
    ng = len(POOL_WINDOWS)
    return pl.pallas_call(
        _pool_kernel,
        grid=(bsz,),
        in_specs=[pl.BlockSpec((1, seq, POOL_WIDTH), lambda b: (b, 0, 0)),
                  pl.BlockSpec((ng, POOL_GROUP_DIM, POOL_GROUP_DIM), lambda b: (0, 0, 0)),
                  pl.BlockSpec((ng, 1, POOL_GROUP_DIM), lambda b: (0, 0, 0)),
                  pl.BlockSpec((ng, 1, POOL_GROUP_DIM), lambda b: (0, 0, 0))],
        out_specs=pl.BlockSpec((1, seq, POOL_WIDTH), lambda b: (b, 0, 0)),
        out_shape=jax.ShapeDtypeStruct((bsz, seq, POOL_WIDTH), BF16),
        scratch_shapes=[pltpu.VMEM((seq + 2 * POOL_PAD, POOL_WIDTH), F32)],
        compiler_params=_cparams(("parallel",)),
        name="pool_mix",
    )(z_lat, pool_w, pool_b.reshape(ng, 1, POOL_GROUP_DIM),
      pool_scale.reshape(ng, 1, POOL_GROUP_DIM))


def _outproj_kernel(x_ref, p_ref, a_ref, w_ref, g_ref, o_ref):
    r = (jnp.dot(p_ref[0], w_ref[pl.ds(0, POOL_WIDTH), :], preferred_element_type=F32)
         + jnp.dot(a_ref[0], w_ref[pl.ds(POOL_WIDTH, a_ref.shape[2]), :],
                   preferred_element_type=F32))
    o_ref[0] = x_ref[0] + g_ref[0] * r


def _outproj_call(x, pool_y, attn, w_out, mod3, tm=512):
    bsz, seq, d = x.shape
    aw = attn.shape[2]
    return pl.pallas_call(
        _outproj_kernel,
        grid=(bsz, seq // tm),
        in_specs=[pl.BlockSpec((1, tm, d), lambda b, i: (b, i, 0)),
                  pl.BlockSpec((1, tm, POOL_WIDTH), lambda b, i: (b, i, 0)),
                  pl.BlockSpec((1, tm, aw), lambda b, i: (b, i, 0)),
                  pl.BlockSpec(w_out.shape, lambda b, i: (0, 0)),
                  pl.BlockSpec((1, 1, d), lambda b, i: (b, 0, 2))],
        out_specs=pl.BlockSpec((1, tm, d), lambda b, i: (b, i, 0)),
        out_shape=jax.ShapeDtypeStruct((bsz, seq, d), F32),
        compiler_params=_cparams(("parallel", "parallel")),
        name="outproj_residual",
    )(x, pool_y, attn, w_out, mod3)


def _top16_rows(s):
    n, t = s.shape
    row = lax.broadcasted_iota(jnp.int32, (n, t), 0).astype(F32)
    row16 = lax.broadcasted_iota(jnp.int32, (PEER_TOPK, t), 0)
    work = s
    rank = jnp.full((n, t), 127.0, F32)
    vals = jnp.zeros((PEER_TOPK, t), F32)
    for i in range(PEER_TOPK):
        m = jnp.max(work, axis=0, keepdims=True)
        idx = jnp.min(jnp.where(work == m, row, float(n)), axis=0, keepdims=True)
        sel = row == idx
        rank = jnp.where(sel, float(i), rank)
        vals = jnp.where(row16 == i, m, vals)
        work = jnp.where(sel, NEG_INF, work)
    return vals, rank


_CAND_SLABS = ((0, 0, 8), (0, 8, 8), (1, 0, 8), (2, 0, 5), (3, 0, 4), (4, 0, 3),
               (5, 0, 2), (6, 0, 2), (7, 0, 2))


def _pair_select(va, vb):
    t = va.shape[1]
    sub = lax.broadcasted_iota(jnp.int32, (8, t), 0)
    subf = sub.astype(F32)
    cands, flats = [], []
    for (i, j0, lim) in _CAND_SLABS:
        c = va[i:i + 1, :] + vb[j0:j0 + 8, :]
        cands.append(jnp.where(sub < lim, c, NEG_INF))
        flats.append(subf + float(i * PEER_TOPK + j0))
    cands.append(va[8:16, :] + vb[0:1, :])
    flats.append(subf * float(PEER_TOPK) + float(8 * PEER_TOPK))
    cand = jnp.concatenate(cands, axis=0)
    flat = jnp.concatenate(flats, axis=0)
    work = cand
    big = float(PEER_TOPK * PEER_TOPK)
    for _ in range(PEER_TOPK):
        m = jnp.max(work, axis=0, keepdims=True)
        idx = jnp.min(jnp.where(work == m, flat, big), axis=0, keepdims=True)
        work = jnp.where(flat == idx, NEG_INF, work)
    sel = jnp.logical_and(work == NEG_INF, cand > NEG_INF)
    self32 = sel.astype(F32)
    top = va[0:1, :] + vb[0:1, :]
    z = jnp.sum(jnp.where(sel, jnp.exp(cand - top), 0.0), axis=0, keepdims=True)
    counts = [jnp.sum(self32[0:16], axis=0, keepdims=True)]
    for k in range(2, 9):
        counts.append(jnp.sum(self32[8 * k:8 * k + 8], axis=0, keepdims=True))
    n = jnp.concatenate(counts + [self32[72:80]], axis=0)
    return n, z


def _sort_network(n):
    size = 16
    pairs = []
    p = 1
    while p < size:
        k = p
        while k >= 1:
            for j in range(k % p, size - k, 2 * k):
                for i in range(min(k, size - j - k)):
                    if (i + j) // (2 * p) == (i + j + k) // (2 * p):
                        pairs.append((i + j, i + j + k))
            k //= 2
        p *= 2
    return [(i, j) for (i, j) in pairs if j < n]


def _sorted_top16(slabs):
    s = list(slabs)
    depth = len(s)
    for i, j in _sort_network(depth):
        s[i], s[j] = jnp.maximum(s[i], s[j]), jnp.minimum(s[i], s[j])
    t = s[0].shape[1]
    sub = lax.broadcasted_iota(jnp.int32, (8, t), 0).astype(F32)
    rows = []
    for i in range(PEER_TOPK):
        m = jnp.max(s[0], axis=0, keepdims=True)
        rows.append(m)
        if i == PEER_TOPK - 1:
            break
        first = jnp.min(jnp.where(s[0] == m, sub, 8.0), axis=0, keepdims=True)
        win = sub == first
        for r in range(min(depth, PEER_TOPK - 1 - i)):
            s[r] = jnp.where(win, s[r + 1] if r + 1 < depth else NEG_INF, s[r])
    return rows


def _tie_rows(s, v):
    cnt = jnp.sum(jnp.where(s >= v[PEER_TOPK - 1], 1.0, 0.0), axis=0, keepdims=True)
    tie = jnp.where(cnt != float(PEER_TOPK), 1.0, 0.0)
    for i in range(PEER_TOPK - 1):
        tie = tie + jnp.where(v[i] == v[i + 1], 1.0, 0.0)
    return tie


def _pair_select_fast(va, vb):
    t = va[0].shape[1]
    row16 = lax.broadcasted_iota(jnp.int32, (PEER_TOPK, t), 0)
    sub = lax.broadcasted_iota(jnp.int32, (8, t), 0)
    va_arr = jnp.zeros((PEER_TOPK, t), F32)
    vb_arr = jnp.zeros((PEER_TOPK, t), F32)
    for i in range(PEER_TOPK):
        va_arr = jnp.where(row16 == i, va[i], va_arr)
        vb_arr = jnp.where(row16 == i, vb[i], vb_arr)
    cands = []
    for (i, j0, lim) in _CAND_SLABS:
        cands.append(jnp.where(sub < lim, va[i] + vb_arr[j0:j0 + 8, :], NEG_INF))
    cands.append(va_arr[8:16, :] + vb[0])
    tau = _sorted_top16(cands)[PEER_TOPK - 1]
    top = va[0] + vb[0]
    sel = [jnp.where(c >= tau, 1.0, 0.0) for c in cands]
    z = sum(jnp.sum(s * jnp.exp(c - top), axis=0, keepdims=True) for s, c in zip(sel, cands))
    n = [jnp.sum(sel[0] + sel[1], axis=0, keepdims=True)]
    for k in range(2, 9):
        n.append(jnp.sum(sel[k], axis=0, keepdims=True))
    for r in range(8):
        n.append(sel[9][r:r + 1, :])
    total = sum(n[:8]) + jnp.sum(sel[9], axis=0, keepdims=True)
    return n, z, jnp.where(total != float(PEER_TOPK), 1.0, 0.0)


def _peer_sel_kernel(x_ref, g_ref, sh_ref, sc_ref, wq_ref, keys_ref,
                     f_ref, na_ref, ea_ref, rb_ref, eb_ref, qt_ref):
    ft = _norm_mod(x_ref[0], g_ref[...], sh_ref[0], sc_ref[0]).T.astype(BF16)
    f_ref[...] = ft
    qt_ref[...] = jnp.dot(wq_ref[...], ft, preferred_element_type=F32)

    def scores(hp):
        k_hi, k_lo = _split_bf16(keys_ref[hp])
        q_hi, q_lo = _split_bf16(qt_ref[pl.ds(pl.multiple_of(hp * PEER_HALF, PEER_HALF),
                                              PEER_HALF), :])
        return (jnp.dot(k_hi, q_hi, preferred_element_type=F32)
                + jnp.dot(k_hi, q_lo, preferred_element_type=F32)
                + jnp.dot(k_lo, q_hi, preferred_element_type=F32))

    def head(h, carry):
        sa = scores(2 * h)
        sb = scores(2 * h + 1)

        va = _sorted_top16([sa[8 * k:8 * k + 8] for k in range(PEER_NKEYS // 8)])
        vb = _sorted_top16([sb[8 * k:8 * k + 8] for k in range(PEER_NKEYS // 8)])
        n_rows, z, tie2 = _pair_select_fast(va, vb)
        na = jnp.zeros_like(sa)
        rank_b = jnp.zeros_like(sb)
        for i in range(PEER_TOPK):
            na = jnp.where(sa == va[i], n_rows[i], na)
            rank_b = jnp.where(vb[i] > sb, float(i + 1), rank_b)
        na_ref[h] = na
        ea_ref[h] = jnp.exp(sa - va[0]) / z
        rb_ref[h] = rank_b.astype(BF16)
        eb_ref[h] = jnp.exp(sb - vb[0]).astype(BF16)

        tie = tie2 + _tie_rows(sa, va) + _tie_rows(sb, vb)

        @pl.when(jnp.max(tie) > 0.0)
        def _():
            va_t, rank_a_t = _top16_rows(sa)
            vb_t, rank_b_t = _top16_rows(sb)
            n_t, z_t = _pair_select(va_t, vb_t)
            na_t = jnp.zeros_like(sa)
            for i in range(PEER_TOPK):
                na_t = jnp.where(rank_a_t == float(i), n_t[i:i + 1, :], na_t)
            na_ref[h] = na_t
            ea_ref[h] = jnp.exp(sa - va_t[0:1, :]) / z_t
            rb_ref[h] = rank_b_t.astype(BF16)
            eb_ref[h] = jnp.exp(sb - vb_t[0:1, :]).astype(BF16)

        return carry

    lax.fori_loop(0, PEER_HEADS, head, 0)


def _peer_sel_call(x1, norm_g, mod3, wq_t, keys, ts=512):
    bsz, seq, d = x1.shape
    tokens = bsz * seq
    nblk = seq // ts
    sel_f32 = jax.ShapeDtypeStruct((PEER_HEADS, PEER_NKEYS, tokens), F32)
    sel_bf16 = jax.ShapeDtypeStruct((PEER_HEADS, PEER_NKEYS, tokens), BF16)
    sel_spec = pl.BlockSpec((PEER_HEADS, PEER_NKEYS, ts), lambda b, i: (0, 0, b * nblk + i))
    return pl.pallas_call(
        _peer_sel_kernel,
        grid=(bsz, nblk),
        in_specs=[pl.BlockSpec((1, ts, d), lambda b, i: (b, i, 0)),
                  pl.BlockSpec((1, d), lambda b, i: (0, 0)),
                  pl.BlockSpec((1, 1, d), lambda b, i: (b, 0, 3)),
                  pl.BlockSpec((1, 1, d), lambda b, i: (b, 0, 4)),
                  pl.BlockSpec(wq_t.shape, lambda b, i: (0, 0)),
                  pl.BlockSpec(keys.shape, lambda b, i: (0, 0, 0))],
        out_specs=[pl.BlockSpec((d, ts), lambda b, i: (0, b * nblk + i)),
                   sel_spec, sel_spec, sel_spec, sel_spec],
        out_shape=[jax.ShapeDtypeStruct((d, tokens), BF16),
                   sel_f32, sel_f32, sel_bf16, sel_bf16],
        scratch_shapes=[pltpu.VMEM((wq_t.shape[0], ts), F32)],
        compiler_params=_cparams(("parallel", "parallel")),
        name="peer_select",
    )(x1, norm_g, mod3, mod3, wq_t, keys)


def _gelu(x):
    return 0.5 * x * (1.0 + lax.erf(x * (1.0 / math.sqrt(2.0))))


def _peer_kernel(f_ref, u_ref, vt_ref, na_ref, ea_ref, rb_ref, eb_ref,
                 x_ref, g2_ref, fg_ref, o_ref, act_ref, p_ref, acc_ref):
    e = pl.program_id(2)
    last = pl.num_programs(2) - 1
    tm = f_ref.shape[1]
    sub = u_ref.shape[0] // 2
    keys_per_sub = sub // PEER_NKEYS
    zero = jnp.zeros((PEER_NKEYS, BF16_TILE_LANES), BF16)

    def activations(j):
        act_ref[j] = jnp.dot(u_ref[pl.ds(j * sub, sub), :], f_ref[...],
                             preferred_element_type=F32)

    def gated(j, n_ref, g_ref, slot):
        for lc in range(tm // BF16_TILE_LANES):
            cols = pl.ds(lc * BF16_TILE_LANES, BF16_TILE_LANES)
            for al in range(keys_per_sub):
                a = j * keys_per_sub + al
                w = zero
                for h in range(PEER_HEADS):
                    na = n_ref[h, pl.ds(a, 1), cols].astype(BF16)
                    ea = g_ref[h, pl.ds(a, 1), cols].astype(BF16)
                    w = w + jnp.where(rb_ref[h, :, cols] < na, eb_ref[h, :, cols], zero) * ea
                rows = pl.ds(al * PEER_NKEYS, PEER_NKEYS)
                g = _gelu(act_ref[j, rows, cols]).astype(BF16)
                p_ref[slot, pl.ds(j * sub + al * PEER_NKEYS, PEER_NKEYS), cols] = g * w

    def values(slot):
        acc_ref[...] += jnp.dot(vt_ref[...], p_ref[slot], preferred_element_type=F32)

    this = e % 2
    before = 1 - this

    @pl.when(e == 0)
    def _():
        acc_ref[...] = jnp.zeros_like(acc_ref)
        activations(0)
        activations(1)
        gated(0, na_ref, ea_ref, this)
        gated(1, na_ref, ea_ref, this)

    @pl.when(jnp.logical_and(e > 0, e < last))
    def _():
        activations(0)
        activations(1)
        values(before)
        gated(0, na_ref, ea_ref, this)
        gated(1, na_ref, ea_ref, this)

    @pl.when(e == last)
    def _():
        values(before)
        y = x_ref[0] + g2_ref[0] * acc_ref[...].T
        ms = jnp.mean(y * y, axis=1, keepdims=True)
        o_ref[0] = y * lax.rsqrt(ms + EPS) * fg_ref[...]


def _peer_call(f, u, v_t, na, ea, rb, eb, x1, mod3, final_g, tm=512, ec=1024):
    bsz, seq, d = x1.shape
    experts = u.shape[0]
    nblk = seq // tm
    nchunk = experts // ec
    a_per_step = ec // PEER_NKEYS
    tok = lambda b, i, e: b * nblk + i
    cur = lambda e: jnp.minimum(e, nchunk - 1)
    prev = lambda e: jnp.maximum(e - 1, 0)
    sel_rows = lambda which: pl.BlockSpec(
        (PEER_HEADS, a_per_step, tm), lambda b, i, e: (0, which(e), tok(b, i, e)))
    sel_all = pl.BlockSpec((PEER_HEADS, PEER_NKEYS, tm), lambda b, i, e: (0, 0, tok(b, i, e)))
    return pl.pallas_call(
        _peer_kernel,
        grid=(bsz, nblk, nchunk + 1),
        in_specs=[pl.BlockSpec((d, tm), lambda b, i, e: (0, tok(b, i, e))),
                  pl.BlockSpec((ec, d), lambda b, i, e: (cur(e), 0)),
                  pl.BlockSpec((d, ec), lambda b, i, e: (0, prev(e))),
                  sel_rows(cur), sel_rows(cur), sel_all, sel_all,
                  pl.BlockSpec((1, tm, d), lambda b, i, e: (b, i, 0)),
                  pl.BlockSpec((1, 1, d), lambda b, i, e: (b, 0, 5)),
                  pl.BlockSpec((1, d), lambda b, i, e: (0, 0))],
        out_specs=pl.BlockSpec((1, tm, d), lambda b, i, e: (b, i, 0)),
        out_shape=jax.ShapeDtypeStruct((bsz, seq, d), F32),
        scratch_shapes=[pltpu.VMEM((2, ec // 2, tm), F32),
                        pltpu.VMEM((2, ec, tm), BF16),
                        pltpu.VMEM((d, tm), F32)],
        compiler_params=_cparams(("parallel", "parallel", "arbitrary")),
        name="peer_dense",
    )(f, u, v_t, na, ea, rb, eb, x1, mod3, final_g)


def _rope_tables(seq):
    pos = np.arange(seq)
    row = (pos // GRID_W).astype(np.float32)
    col = (pos % GRID_W).astype(np.float32)
    inv_freq = (ROPE_BASE ** (-np.arange(ROPE_HALF, dtype=np.float32) / ROPE_HALF)).astype(np.float32)
    lane = np.arange(LANES)
    axis = (lane % DIFF_QKDIM) // ROPE_AXIS_DIM
    freq = inv_freq[lane % ROPE_HALF]
    p = np.where(axis[None, :] == 0, row[:, None], col[:, None]).astype(np.float32)
    ang = p * freq[None, :]
    sign = np.where((lane % ROPE_AXIS_DIM) < ROPE_HALF, -1.0, 1.0).astype(np.float32)
    return jnp.asarray(np.cos(ang), F32), jnp.asarray(np.sin(ang) * sign[None, :], F32)


def kernel(x, c, ctx, c_ctx, ada_w, ada_b, norm1_g, w_in, pool_w, pool_b, pool_scale,
           diff_lambda, subln_g, w_out, norm2_g, peer_wq, peer_keys, peer_u, peer_v, final_g):
    bsz, seq, d = x.shape
    ctx_len = ctx.shape[1]
    layer = 0
    lam_init = 0.8 - 0.6 * math.exp(-0.3 * layer)
    mod_rows = 8
    assert bsz + 1 <= mod_rows

    cc = jnp.concatenate([c, c_ctx[None, :], jnp.zeros((mod_rows - bsz - 1, d), F32)], axis=0)
    mod = _ada_call(cc, ada_w[layer], ada_b[layer][None, :])
    mod3 = mod.reshape(mod_rows, 1, 6 * d)

    w_in_b = w_in[layer].astype(BF16)
    g1 = norm1_g[layer][None, :]
    cos, sin = _rope_tables(seq)
    in_width = w_in_b.shape[1]
    attn_width = (in_width - POOL_WIDTH) // 3
    z_lat = _inproj_call(x, g1, mod3, lambda b: b, w_in_b, cos, sin,
                         rope=True, col0=0, ncols=in_width)
    kv_ctx = _inproj_call(ctx, g1, mod3, lambda b: bsz, w_in_b, cos, sin,
                          rope=False, col0=POOL_WIDTH + attn_width, ncols=2 * attn_width)

    attn, u_b, v_t, wq_t, w_out_b = _attn_call(
        z_lat, kv_ctx, diff_lambda[layer], subln_g[layer][None, :],
        peer_u[layer], peer_v[layer], peer_wq[layer], w_out[layer], lam_init=lam_init)
    pool_y = _pool_call(z_lat, pool_w[layer], pool_b[layer], pool_scale[layer])
    x1 = _outproj_call(x, pool_y, attn, w_out_b, mod3)

    keys = peer_keys[layer].reshape(2 * PEER_HEADS, PEER_NKEYS, PEER_HALF)
    f, na, ea, rb, eb = _peer_sel_call(x1, norm2_g[layer][None, :], mod3, wq_t, keys)
    return _peer_call(f, u_b, v_t, na, ea, rb, eb, x1, mod3, final_g[None, :])
```

```python
import functools
import math

import jax
import jax.numpy as jnp
import numpy as np
from jax import lax
from jax.experimental import pallas as pl
from jax.experimental.pallas import tpu as pltpu

F32 = jnp.float32
BF16 = jnp.bfloat16

EPS = 1e-6
GRID_W = 64
POOL_WINDOWS = (2, 4, 8, 16)
POOL_GROUP_DIM = 128
POOL_WIDTH = POOL_GROUP_DIM * len(POOL_WINDOWS)
DIFF_VDIM = 128
DIFF_HEADS = 12
DIFF_QKDIM = 64
ROPE_BASE = 10000.0
ROPE_AXIS_DIM = 32
ROPE_HALF = ROPE_AXIS_DIM // 2
PEER_HEADS = 8
PEER_NKEYS = 128
PEER_HALF = 128
PEER_TOPK = 16
LANES = 128
BF16_TILE_LANES = 256
SOFTMAX_ROWS = 16
POOL_PAD = 8
VMEM_LIMIT_BYTES = 56 * 1024 * 1024
NEG_INF = float("-inf")


def _cparams(sem):
    return pltpu.CompilerParams(dimension_semantics=sem, vmem_limit_bytes=VMEM_LIMIT_BYTES)


def _split_bf16(a):
    hi = a.astype(BF16)
    lo = (a - hi.astype(F32)).astype(BF16)
    return hi, lo


def _ada_kernel(c_ref, w_ref, b_ref, win_ref, o_ref, winb_ref):
    c = c_ref[...]
    s = c * (1.0 / (1.0 + jnp.exp(-c)))
    o_ref[...] = jnp.dot(s.astype(BF16), w_ref[...].astype(BF16),
                         preferred_element_type=F32) + b_ref[...]
    winb_ref[...] = win_ref[...].astype(BF16)


def _ada_call(cc, w, b, w_in, tn=1536):
    rows, d = cc.shape
    n = w.shape[1]
    steps = n // tn
    wrows = w_in.shape[0] // steps
    assert steps * tn == n and wrows * steps == w_in.shape[0]
    return pl.pallas_call(
        _ada_kernel,
        grid=(steps,),
        in_specs=[pl.BlockSpec((rows, d), lambda j: (0, 0)),
                  pl.BlockSpec((d, tn), lambda j: (0, j)),
                  pl.BlockSpec((1, tn), lambda j: (0, j)),
                  pl.BlockSpec((wrows, w_in.shape[1]), lambda j: (j, 0))],
        out_specs=[pl.BlockSpec((rows, tn), lambda j: (0, j)),
                   pl.BlockSpec((wrows, w_in.shape[1]), lambda j: (j, 0))],
        out_shape=[jax.ShapeDtypeStruct((rows, n), F32),
                   jax.ShapeDtypeStruct(w_in.shape, BF16)],
        compiler_params=_cparams(("arbitrary",)),
        name="ada_mod",
    )(cc, w, b, w_in)


def _norm_mod(xf, g, sh, sc):
    ms = jnp.mean(xf * xf, axis=-1, keepdims=True)
    y = xf * lax.rsqrt(ms + EPS) * g
    return y * (1.0 + sc) + sh


def _inproj_kernel(x_ref, g_ref, sh_ref, sc_ref, w_ref, cos_ref, sin_ref, o_ref, h_ref, z_ref,
                   *, rope, col0, tn):
    n_tiles = o_ref.shape[2] // tn
    q_tiles = (DIFF_VDIM * DIFF_HEADS) // tn
    pool_tiles = POOL_WIDTH // tn

    def matmul(n):
        z_ref[n % 2] = jnp.dot(h_ref[...], w_ref[:, pl.ds(col0 + n * tn, tn)],
                               preferred_element_type=F32)

    def epilogue(n):
        z = z_ref[n % 2]
        if rope and pool_tiles <= n < pool_tiles + 2 * q_tiles:
            reps = tn // LANES
            cos = jnp.concatenate([cos_ref[...]] * reps, axis=1)
            sin = jnp.concatenate([sin_ref[...]] * reps, axis=1)
            lane = lax.broadcasted_iota(jnp.int32, z.shape, 1)
            first = (lane % ROPE_AXIS_DIM) < ROPE_HALF
            partner = jnp.where(first, pltpu.roll(z, tn - ROPE_HALF, 1),
                                pltpu.roll(z, ROPE_HALF, 1))
            z = z * cos + partner * sin
            if n < pool_tiles + q_tiles:
                z = z * DIFF_QKDIM ** -0.5
        o_ref[0, :, pl.ds(n * tn, tn)] = z.astype(BF16)

    h = _norm_mod(x_ref[0], g_ref[...], sh_ref[0], sc_ref[0])
    h_ref[...] = h.astype(BF16)
    for stage in range(n_tiles + 1):
        if stage < n_tiles:
            matmul(stage)
        if stage >= 1:
            epilogue(stage - 1)


def _inproj_call(x, g, mod3, mod_row_fn, w, cos, sin, *, rope, col0, ncols, tm=512, tn=512):
    bsz, rows, d = x.shape
    tm = min(tm, rows)
    kern = functools.partial(_inproj_kernel, rope=rope, col0=col0, tn=tn)
    return pl.pallas_call(
        kern,
        grid=(bsz, rows // tm),
        in_specs=[pl.BlockSpec((1, tm, d), lambda b, i: (b, i, 0)),
                  pl.BlockSpec((1, d), lambda b, i: (0, 0)),
                  pl.BlockSpec((1, 1, d), lambda b, i: (mod_row_fn(b), 0, 0)),
                  pl.BlockSpec((1, 1, d), lambda b, i: (mod_row_fn(b), 0, 1)),
                  pl.BlockSpec(w.shape, lambda b, i: (0, 0), pipeline_mode=pl.Buffered(1)),
                  pl.BlockSpec((tm, LANES), lambda b, i: (i, 0)),
                  pl.BlockSpec((tm, LANES), lambda b, i: (i, 0))],
        out_specs=pl.BlockSpec((1, tm, ncols), lambda b, i: (b, i, 0)),
        out_shape=jax.ShapeDtypeStruct((bsz, rows, ncols), BF16),
        scratch_shapes=[pltpu.VMEM((tm, d), BF16),
                        pltpu.VMEM((2, tm, tn), F32)],
        compiler_params=_cparams(("parallel", "arbitrary")),
        name="inproj_rope" if rope else "inproj_ctx",
    )(x, g, mod3, mod3, w, cos, sin)


def _attn_kernel(q_ref, kc_ref, vc_ref, kl_ref, vl_ref, dl_ref, sg_ref, u_ref, v_ref, wq_ref,
                 wo_ref, o_ref, ub_ref, vt_ref, wqt_ref, wob_ref, s_ref, e_ref, va_ref,
                 *, lam_init, sub):
    ctx_len = kc_ref.shape[1]
    seq = kl_ref.shape[1]
    n_sub = q_ref.shape[1] // sub
    lq = dl_ref[...]
    lam = (jnp.exp(jnp.sum(lq[0:1] * lq[1:2], axis=1, keepdims=True))
           - jnp.exp(jnp.sum(lq[2:3] * lq[3:4], axis=1, keepdims=True)) + lam_init)
    nt = (((1,), (1,)), ((), ()))
    lane = lax.broadcasted_iota(jnp.int32, (sub, LANES), 1)
    zero = jnp.zeros((sub, LANES), BF16)

    ones_col = jnp.where(lax.broadcasted_iota(jnp.int32, (ctx_len + seq, LANES), 1) == 0,
                         1.0, 0.0).astype(BF16)
    va_ref[pl.ds(0, ctx_len), pl.ds(0, LANES)] = vc_ref[0]
    va_ref[pl.ds(ctx_len, seq), pl.ds(0, LANES)] = vl_ref[0]
    va_ref[:, pl.ds(LANES, LANES)] = ones_col

    def scores(t):
        q = q_ref[0, pl.ds(t * sub, sub), :]
        q2 = jnp.concatenate([jnp.where(lane < DIFF_QKDIM, q, zero),
                              jnp.where(lane >= DIFF_QKDIM, q, zero)], axis=0)
        s_ref[t % 2, :, pl.ds(0, ctx_len)] = lax.dot_general(
            q2, kc_ref[0], nt, preferred_element_type=F32)
        s_ref[t % 2, :, pl.ds(ctx_len, seq)] = lax.dot_general(
            q2, kl_ref[0], nt, preferred_element_type=F32)

    def numerators(t):
        for r in range(0, 2 * sub, SOFTMAX_ROWS):
            rows = pl.ds(r, SOFTMAX_ROWS)
            s = s_ref[t % 2, rows, :]
            e_ref[t % 2, rows, :] = jnp.exp(s - jnp.max(s, axis=1, keepdims=True)).astype(BF16)

    def outputs(t):
        oa = jnp.dot(e_ref[t % 2], va_ref[...], preferred_element_type=F32)
        o2 = oa[:, :LANES] / oa[:, LANES:LANES + 1]
        o = o2[:sub] - lam * o2[sub:]
        ms = jnp.mean(o * o, axis=1, keepdims=True)
        y = o * lax.rsqrt(ms + EPS) * sg_ref[...]
        o_ref[0, pl.ds(t * sub, sub), :] = (y * (1.0 - lam_init)).astype(BF16)

    for stage in range(n_sub + 2):
        if stage < n_sub:
            scores(stage)
        if 1 <= stage <= n_sub:
            numerators(stage - 1)
        if stage >= 2:
            outputs(stage - 2)

    ub_ref[...] = u_ref[...].astype(BF16)
    vt_ref[...] = v_ref[...].T.astype(BF16)
    wqt_ref[...] = wq_ref[...].T.astype(BF16)
    wob_ref[...] = wo_ref[...].astype(BF16)


def _attn_call(z_lat, kv_ctx, diff_lambda, subln_g, peer_u, peer_v, peer_wq, w_out, *, lam_init,
               tq=2048, sub=256):
    bsz, seq, _ = z_lat.shape
    ctx_len = kv_ctx.shape[1]
    tq = min(tq, seq)
    heads = DIFF_HEADS
    qb = POOL_WIDTH // LANES
    kb = qb + heads
    vb = kb + heads
    experts, d = peer_u.shape
    side_heads = 8
    trows = experts // (bsz * side_heads)
    assert trows * bsz * side_heads == experts
    tblk = lambda b, h, i: b * side_heads + jnp.minimum(h, side_heads - 1)
    rest = heads - side_heads
    wrows = peer_wq.shape[0] // (bsz * rest)
    assert wrows * bsz * rest == peer_wq.shape[0] == w_out.shape[0] and wrows % LANES == 0
    wblk = lambda b, h, i: b * rest + jnp.maximum(h - side_heads, 0)
    kern = functools.partial(_attn_kernel, lam_init=lam_init, sub=sub)
    return pl.pallas_call(
        kern,
        grid=(bsz, heads, seq // tq),
        in_specs=[pl.BlockSpec((1, tq, LANES), lambda b, h, i: (b, i, qb + h)),
                  pl.BlockSpec((1, ctx_len, LANES), lambda b, h, i: (b, 0, h)),
                  pl.BlockSpec((1, ctx_len, LANES), lambda b, h, i: (b, 0, heads + h)),
                  pl.BlockSpec((1, seq, LANES), lambda b, h, i: (b, 0, kb + h)),
                  pl.BlockSpec((1, seq, LANES), lambda b, h, i: (b, 0, vb + h)),
                  pl.BlockSpec(diff_lambda.shape, lambda b, h, i: (0, 0)),
                  pl.BlockSpec((1, LANES), lambda b, h, i: (0, 0)),
                  pl.BlockSpec((trows, d), lambda b, h, i: (tblk(b, h, i), 0)),
                  pl.BlockSpec((trows, d), lambda b, h, i: (tblk(b, h, i), 0)),
                  pl.BlockSpec((wrows, peer_wq.shape[1]), lambda b, h, i: (wblk(b, h, i), 0)),
                  pl.BlockSpec((wrows, w_out.shape[1]), lambda b, h, i: (wblk(b, h, i), 0))],
        out_specs=[pl.BlockSpec((1, tq, LANES), lambda b, h, i: (b, i, h)),
                   pl.BlockSpec((trows, d), lambda b, h, i: (tblk(b, h, i), 0)),
                   pl.BlockSpec((d, trows), lambda b, h, i: (0, tblk(b, h, i))),
                   pl.BlockSpec((peer_wq.shape[1], wrows), lambda b, h, i: (0, wblk(b, h, i))),
                   pl.BlockSpec((wrows, w_out.shape[1]), lambda b, h, i: (wblk(b, h, i), 0))],
        out_shape=[jax.ShapeDtypeStruct((bsz, seq, heads * DIFF_VDIM), BF16),
                   jax.ShapeDtypeStruct((experts, d), BF16),
                   jax.ShapeDtypeStruct((d, experts), BF16),
                   jax.ShapeDtypeStruct(peer_wq.shape[::-1], BF16),
                   jax.ShapeDtypeStruct(w_out.shape, BF16)],
        scratch_shapes=[pltpu.VMEM((2, 2 * sub, ctx_len + seq), F32),
                        pltpu.VMEM((2, 2 * sub, ctx_len + seq), BF16),
                        pltpu.VMEM((ctx_len + seq, 2 * LANES), BF16)],
        compiler_params=_cparams(("parallel", "arbitrary", "arbitrary")),
        name="diff_attn",
    )(z_lat, kv_ctx, kv_ctx, z_lat, z_lat, diff_lambda, subln_g, peer_u, peer_v, peer_wq, w_out)


def _pool_kernel(z_ref, w_ref, b_ref, s_ref, o_ref, zp_ref):
    seq = z_ref.shape[1]
    zp_ref[...] = jnp.zeros_like(zp_ref)
    zp_ref[pl.ds(POOL_PAD, seq), :] = z_ref[0].astype(F32)
    t = lax.broadcasted_iota(jnp.int32, (seq, POOL_GROUP_DIM), 0)
    outs = []
    for g, win in enumerate(POOL_WINDOWS):
        half = win // 2
        cols = pl.ds(g * POOL_GROUP_DIM, POOL_GROUP_DIM)
        acc = jnp.zeros((seq, POOL_GROUP_DIM), F32)
        for k in range(-half, half):
            acc = acc + zp_ref[pl.ds(POOL_PAD + k, seq), cols]
        cnt = (jnp.minimum(t + half, seq) - jnp.maximum(t - half, 0)).astype(F32)
        y = acc / cnt - zp_ref[pl.ds(POOL_PAD, seq), cols]
        r = jnp.dot(y.astype(BF16), w_ref[g].astype(BF16), preferred_element_type=F32)
        outs.append((r + b_ref[g]) * s_ref[g])
    o_ref[0] = jnp.concatenate(outs, axis=1).astype(BF16)


def _pool_call(z_lat, pool_w, pool_b, pool_scale):
    bsz, seq, _ = z_lat.shape
    ng = len(POOL_WINDOWS)
    return pl.pallas_call(
        _pool_kernel,
        grid=(bsz,),
        in_specs=[pl.BlockSpec((1, seq, POOL_WIDTH), lambda b: (b, 0, 0)),
                  pl.BlockSpec((ng, POOL_GROUP_DIM, POOL_GROUP_DIM), lambda b: (0, 0, 0)),
                  pl.BlockSpec((ng, 1, POOL_GROUP_DIM), lambda b: (0, 0, 0)),
                  pl.BlockSpec((ng, 1, POOL_GROUP_DIM), lambda b: (0, 0, 0))],
        out_specs=pl.BlockSpec((1, seq, POOL_WIDTH), lambda b: (b, 0, 0)),
        out_shape=jax.ShapeDtypeStruct((bsz, seq, POOL_WIDTH), BF16),
        scratch_shapes=[pltpu.VMEM((seq + 2 * POOL_PAD, POOL_WIDTH), F32)],
        compiler_params=_cparams(("parallel",)),
        name="pool_mix",
    )(z_lat, pool_w, pool_b.reshape(ng, 1, POOL_GROUP_DIM),
      pool_scale.reshape(ng, 1, POOL_GROUP_DIM))


def _outproj_kernel(x_ref, p_ref, a_ref, w_ref, g_ref, o_ref):
    r = (jnp.dot(p_ref[0], w_ref[pl.ds(0, POOL_WIDTH), :], preferred_element_type=F32)
         + jnp.dot(a_ref[0], w_ref[pl.ds(POOL_WIDTH, a_ref.shape[2]), :],
                   preferred_element_type=F32))
    o_ref[0] = x_ref[0] + g_ref[0] * r


def _outproj_call(x, pool_y, attn, w_out, mod3, tm=512):
    bsz, seq, d = x.shape
    aw = attn.shape[2]
    return pl.pallas_call(
        _outproj_kernel,
        grid=(bsz, seq // tm),
        in_specs=[pl.BlockSpec((1, tm, d), lambda b, i: (b, i, 0)),
                  pl.BlockSpec((1, tm, POOL_WIDTH), lambda b, i: (b, i, 0)),
                  pl.BlockSpec((1, tm, aw), lambda b, i: (b, i, 0)),
                  pl.BlockSpec(w_out.shape, lambda b, i: (0, 0)),
                  pl.BlockSpec((1, 1, d), lambda b, i: (b, 0, 2))],
        out_specs=pl.BlockSpec((1, tm, d), lambda b, i: (b, i, 0)),
        out_shape=jax.ShapeDtypeStruct((bsz, seq, d), F32),
        compiler_params=_cparams(("parallel", "parallel")),
        name="outproj_residual",
    )(x, pool_y, attn, w_out, mod3)


def _top16_rows(s):
    n, t = s.shape
    row = lax.broadcasted_iota(jnp.int32, (n, t), 0).astype(F32)
    row16 = lax.broadcasted_iota(jnp.int32, (PEER_TOPK, t), 0)
    work = s
    rank = jnp.full((n, t), 127.0, F32)
    vals = jnp.zeros((PEER_TOPK, t), F32)
    for i in range(PEER_TOPK):
        m = jnp.max(work, axis=0, keepdims=True)
        idx = jnp.min(jnp.where(work == m, row, float(n)), axis=0, keepdims=True)
        sel = row == idx
        rank = jnp.where(sel, float(i), rank)
        vals = jnp.where(row16 == i, m, vals)
        work = jnp.where(sel, NEG_INF, work)
    return vals, rank


_CAND_SLABS = ((0, 0, 8), (0, 8, 8), (1, 0, 8), (2, 0, 5), (3, 0, 4), (4, 0, 3),
               (5, 0, 2), (6, 0, 2), (7, 0, 2))


def _pair_select(va, vb):
    t = va.shape[1]
    sub = lax.broadcasted_iota(jnp.int32, (8, t), 0)
    subf = sub.astype(F32)
    cands, flats = [], []
    for (i, j0, lim) in _CAND_SLABS:
        c = va[i:i + 1, :] + vb[j0:j0 + 8, :]
        cands.append(jnp.where(sub < lim, c, NEG_INF))
        flats.append(subf + float(i * PEER_TOPK + j0))
    cands.append(va[8:16, :] + vb[0:1, :])
    flats.append(subf * float(PEER_TOPK) + float(8 * PEER_TOPK))
    cand = jnp.concatenate(cands, axis=0)
    flat = jnp.concatenate(flats, axis=0)
    work = cand
    big = float(PEER_TOPK * PEER_TOPK)
    for _ in range(PEER_TOPK):
        m = jnp.max(work, axis=0, keepdims=True)
        idx = jnp.min(jnp.where(work == m, flat, big), axis=0, keepdims=True)
        work = jnp.where(flat == idx, NEG_INF, work)
    sel = jnp.logical_and(work == NEG_INF, cand > NEG_INF)
    self32 = sel.astype(F32)
    top = va[0:1, :] + vb[0:1, :]
    z = jnp.sum(jnp.where(sel, jnp.exp(cand - top), 0.0), axis=0, keepdims=True)
    counts = [jnp.sum(self32[0:16], axis=0, keepdims=True)]
    for k in range(2, 9):
        counts.append(jnp.sum(self32[8 * k:8 * k + 8], axis=0, keepdims=True))
    n = jnp.concatenate(counts + [self32[72:80]], axis=0)
    return n, z


def _sort_network(n):
    size = 16
    pairs = []
    p = 1
    while p < size:
        k = p
        while k >= 1:
            for j in range(k % p, size - k, 2 * k):
                for i in range(min(k, size - j - k)):
                    if (i + j) // (2 * p) == (i + j + k) // (2 * p):
                        pairs.append((i + j, i + j + k))
            k //= 2
        p *= 2
    return [(i, j) for (i, j) in pairs if j < n]


def _sorted_top16(slabs):
    s = list(slabs)
    depth = len(s)
    for i, j in _sort_network(depth):
        s[i], s[j] = jnp.maximum(s[i], s[j]), jnp.minimum(s[i], s[j])
    t = s[0].shape[1]
    sub = lax.broadcasted_iota(jnp.int32, (8, t), 0).astype(F32)
    rows = []
    for i in range(PEER_TOPK):
        m = jnp.max(s[0], axis=0, keepdims=True)
        rows.append(m)
        if i == PEER_TOPK - 1:
            break
        first = jnp.min(jnp.where(s[0] == m, sub, 8.0), axis=0, keepdims=True)
        win = sub == first
        for r in range(min(depth, PEER_TOPK - 1 - i)):
            s[r] = jnp.where(win, s[r + 1] if r + 1 < depth else NEG_INF, s[r])
    return rows


def _tie_rows(s, v):
    cnt = jnp.sum(jnp.where(s >= v[PEER_TOPK - 1], 1.0, 0.0), axis=0, keepdims=True)
    tie = jnp.where(cnt != float(PEER_TOPK), 1.0, 0.0)
    for i in range(PEER_TOPK - 1):
        tie = tie + jnp.where(v[i] == v[i + 1], 1.0, 0.0)
    return tie


def _pair_select_fast(va, vb):
    t = va[0].shape[1]
    row16 = lax.broadcasted_iota(jnp.int32, (PEER_TOPK, t), 0)
    sub = lax.broadcasted_iota(jnp.int32, (8, t), 0)
    va_arr = jnp.zeros((PEER_TOPK, t), F32)
    vb_arr = jnp.zeros((PEER_TOPK, t), F32)
    for i in range(PEER_TOPK):
        va_arr = jnp.where(row16 == i, va[i], va_arr)
        vb_arr = jnp.where(row16 == i, vb[i], vb_arr)
    cands = []
    for (i, j0, lim) in _CAND_SLABS:
        cands.append(jnp.where(sub < lim, va[i] + vb_arr[j0:j0 + 8, :], NEG_INF))
    cands.append(va_arr[8:16, :] + vb[0])
    tau = _sorted_top16(cands)[PEER_TOPK - 1]
    top = va[0] + vb[0]
    sel = [jnp.where(c >= tau, 1.0, 0.0) for c in cands]
    z = sum(jnp.sum(s * jnp.exp(c - top), axis=0, keepdims=True) for s, c in zip(sel, cands))
    n = [jnp.sum(sel[0] + sel[1], axis=0, keepdims=True)]
    for k in range(2, 9):
        n.append(jnp.sum(sel[k], axis=0, keepdims=True))
    for r in range(8):
        n.append(sel[9][r:r + 1, :])
    total = sum(n[:8]) + jnp.sum(sel[9], axis=0, keepdims=True)
    return n, z, jnp.where(total != float(PEER_TOPK), 1.0, 0.0)


def _peer_sel_kernel(x_ref, g_ref, sh_ref, sc_ref, wq_ref, keys_ref,
                     f_ref, na_ref, ea_ref, rb_ref, eb_ref, qt_ref):
    ft = _norm_mod(x_ref[0], g_ref[...], sh_ref[0], sc_ref[0]).T.astype(BF16)
    f_ref[...] = ft
    qt_ref[...] = jnp.dot(wq_ref[...], ft, preferred_element_type=F32)

    def scores(hp):
        k_hi, k_lo = _split_bf16(keys_ref[hp])
        q_hi, q_lo = _split_bf16(qt_ref[pl.ds(pl.multiple_of(hp * PEER_HALF, PEER_HALF),
                                              PEER_HALF), :])
        return (jnp.dot(k_hi, q_hi, preferred_element_type=F32)
                + jnp.dot(k_hi, q_lo, preferred_element_type=F32)
                + jnp.dot(k_lo, q_hi, preferred_element_type=F32))

    def head(h, carry):
        sa = scores(2 * h)
        sb = scores(2 * h + 1)

        va = _sorted_top16([sa[8 * k:8 * k + 8] for k in range(PEER_NKEYS // 8)])
        vb = _sorted_top16([sb[8 * k:8 * k + 8] for k in range(PEER_NKEYS // 8)])
        n_rows, z, tie2 = _pair_select_fast(va, vb)
        na = jnp.zeros_like(sa)
        rank_b = jnp.zeros_like(sb)
        for i in range(PEER_TOPK):
            na = jnp.where(sa == va[i], n_rows[i], na)
            rank_b = jnp.where(vb[i] > sb, float(i + 1), rank_b)
        na_ref[h] = na
        ea_ref[h] = jnp.exp(sa - va[0]) / z
        rb_ref[h] = rank_b.astype(BF16)
        eb_ref[h] = jnp.exp(sb - vb[0]).astype(BF16)

        tie = tie2 + _tie_rows(sa, va) + _tie_rows(sb, vb)

        @pl.when(jnp.max(tie) > 0.0)
        def _():
            va_t, rank_a_t = _top16_rows(sa)
            vb_t, rank_b_t = _top16_rows(sb)
            n_t, z_t = _pair_select(va_t, vb_t)
            na_t = jnp.zeros_like(sa)
            for i in range(PEER_TOPK):
                na_t = jnp.where(rank_a_t == float(i), n_t[i:i + 1, :], na_t)
            na_ref[h] = na_t
            ea_ref[h] = jnp.exp(sa - va_t[0:1, :]) / z_t
            rb_ref[h] = rank_b_t.astype(BF16)
            eb_ref[h] = jnp.exp(sb - vb_t[0:1, :]).astype(BF16)

        return carry

    lax.fori_loop(0, PEER_HEADS, head, 0)


def _peer_sel_call(x1, norm_g, mod3, wq_t, keys, ts=512):
    bsz, seq, d = x1.shape
    tokens = bsz * seq
    nblk = seq // ts
    sel_f32 = jax.ShapeDtypeStruct((PEER_HEADS, PEER_NKEYS, tokens), F32)
    sel_bf16 = jax.ShapeDtypeStruct((PEER_HEADS, PEER_NKEYS, tokens), BF16)
    sel_spec = pl.BlockSpec((PEER_HEADS, PEER_NKEYS, ts), lambda b, i: (0, 0, b * nblk + i))
    return pl.pallas_call(
        _peer_sel_kernel,
        grid=(bsz, nblk),
        in_specs=[pl.BlockSpec((1, ts, d), lambda b, i: (b, i, 0)),
                  pl.BlockSpec((1, d), lambda b, i: (0, 0)),
                  pl.BlockSpec((1, 1, d), lambda b, i: (b, 0, 3)),
                  pl.BlockSpec((1, 1, d), lambda b, i: (b, 0, 4)),
                  pl.BlockSpec(wq_t.shape, lambda b, i: (0, 0)),
                  pl.BlockSpec(keys.shape, lambda b, i: (0, 0, 0))],
        out_specs=[pl.BlockSpec((d, ts), lambda b, i: (0, b * nblk + i)),
                   sel_spec, sel_spec, sel_spec, sel_spec],
        out_shape=[jax.ShapeDtypeStruct((d, tokens), BF16),
                   sel_f32, sel_f32, sel_bf16, sel_bf16],
        scratch_shapes=[pltpu.VMEM((wq_t.shape[0], ts), F32)],
        compiler_params=_cparams(("parallel", "parallel")),
        name="peer_select",
    )(x1, norm_g, mod3, mod3, wq_t, keys)


def _gelu(x):
    return 0.5 * x * (1.0 + lax.erf(x * (1.0 / math.sqrt(2.0))))


def _peer_kernel(f_ref, u_ref, vt_ref, na_ref, ea_ref, rb_ref, eb_ref,
                 x_ref, g2_ref, fg_ref, o_ref, act_ref, p_ref, acc_ref):
    e = pl.program_id(2)
    last = pl.num_programs(2) - 1
    tm = f_ref.shape[1]
    sub = u_ref.shape[0] // 2
    keys_per_sub = sub // PEER_NKEYS
    zero = jnp.zeros((PEER_NKEYS, BF16_TILE_LANES), BF16)

    def activations(j):
        act_ref[j] = jnp.dot(u_ref[pl.ds(j * sub, sub), :], f_ref[...],
                             preferred_element_type=F32)

    def gated(j, n_ref, g_ref, slot):
        for lc in range(tm // BF16_TILE_LANES):
            cols = pl.ds(lc * BF16_TILE_LANES, BF16_TILE_LANES)
            for al in range(keys_per_sub):
                a = j * keys_per_sub + al
                w = zero
                for h in range(PEER_HEADS):
                    na = n_ref[h, pl.ds(a, 1), cols].astype(BF16)
                    ea = g_ref[h, pl.ds(a, 1), cols].astype(BF16)
                    w = w + jnp.where(rb_ref[h, :, cols] < na, eb_ref[h, :, cols], zero) * ea
                rows = pl.ds(al * PEER_NKEYS, PEER_NKEYS)
                g = _gelu(act_ref[j, rows, cols]).astype(BF16)
                p_ref[slot, pl.ds(j * sub + al * PEER_NKEYS, PEER_NKEYS), cols] = g * w

    def values(slot):
        acc_ref[...] += jnp.dot(vt_ref[...], p_ref[slot], preferred_element_type=F32)

    this = e % 2
    before = 1 - this

    @pl.when(e == 0)
    def _():
        acc_ref[...] = jnp.zeros_like(acc_ref)
        activations(0)
        activations(1)
        gated(0, na_ref, ea_ref, this)
        gated(1, na_ref, ea_ref, this)

    @pl.when(jnp.logical_and(e > 0, e < last))
    def _():
        activations(0)
        activations(1)
        values(before)
        gated(0, na_ref, ea_ref, this)
        gated(1, na_ref, ea_ref, this)

    @pl.when(e == last)
    def _():
        values(before)
        y = x_ref[0] + g2_ref[0] * acc_ref[...].T
        ms = jnp.mean(y * y, axis=1, keepdims=True)
        o_ref[0] = y * lax.rsqrt(ms + EPS) * fg_ref[...]


def _peer_call(f, u, v_t, na, ea, rb, eb, x1, mod3, final_g, tm=512, ec=1024):
    bsz, seq, d = x1.shape
    experts = u.shape[0]
    nblk = seq // tm
    nchunk = experts // ec
    a_per_step = ec // PEER_NKEYS
    tok = lambda b, i, e: b * nblk + i
    cur = lambda e: jnp.minimum(e, nchunk - 1)
    prev = lambda e: jnp.maximum(e - 1, 0)
    sel_rows = lambda which: pl.BlockSpec(
        (PEER_HEADS, a_per_step, tm), lambda b, i, e: (0, which(e), tok(b, i, e)))
    sel_all = pl.BlockSpec((PEER_HEADS, PEER_NKEYS, tm), lambda b, i, e: (0, 0, tok(b, i, e)))
    return pl.pallas_call(
        _peer_kernel,
        grid=(bsz, nblk, nchunk + 1),
        in_specs=[pl.BlockSpec((d, tm), lambda b, i, e: (0, tok(b, i, e))),
                  pl.BlockSpec((ec, d), lambda b, i, e: (cur(e), 0)),
                  pl.BlockSpec((d, ec), lambda b, i, e: (0, prev(e))),
                  sel_rows(cur), sel_rows(cur), sel_all, sel_all,
                  pl.BlockSpec((1, tm, d), lambda b, i, e: (b, i, 0)),
                  pl.BlockSpec((1, 1, d), lambda b, i, e: (b, 0, 5)),
                  pl.BlockSpec((1, d), lambda b, i, e: (0, 0))],
        out_specs=pl.BlockSpec((1, tm, d), lambda b, i, e: (b, i, 0)),
        out_shape=jax.ShapeDtypeStruct((bsz, seq, d), F32),
        scratch_shapes=[pltpu.VMEM((2, ec // 2, tm), F32),
                        pltpu.VMEM((2, ec, tm), BF16),
                        pltpu.VMEM((d, tm), F32)],
        compiler_params=_cparams(("parallel", "parallel", "arbitrary")),
        name="peer_dense",
    )(f, u, v_t, na, ea, rb, eb, x1, mod3, final_g)


def _rope_tables(seq):
    pos = np.arange(seq)
    row = (pos // GRID_W).astype(np.float32)
    col = (pos % GRID_W).astype(np.float32)
    inv_freq = (ROPE_BASE ** (-np.arange(ROPE_HALF, dtype=np.float32) / ROPE_HALF)).astype(np.float32)
    lane = np.arange(LANES)
    axis = (lane % DIFF_QKDIM) // ROPE_AXIS_DIM
    freq = inv_freq[lane % ROPE_HALF]
    p = np.where(axis[None, :] == 0, row[:, None], col[:, None]).astype(np.float32)
    ang = p * freq[None, :]
    sign = np.where((lane % ROPE_AXIS_DIM) < ROPE_HALF, -1.0, 1.0).astype(np.float32)
    return jnp.asarray(np.cos(ang), F32), jnp.asarray(np.sin(ang) * sign[None, :], F32)


def kernel(x, c, ctx, c_ctx, ada_w, ada_b, norm1_g, w_in, pool_w, pool_b, pool_scale,
           diff_lambda, subln_g, w_out, norm2_g, peer_wq, peer_keys, peer_u, peer_v, final_g):
    bsz, seq, d = x.shape
    ctx_len = ctx.shape[1]
    layer = 0
    lam_init = 0.8 - 0.6 * math.exp(-0.3 * layer)
    mod_rows = 8
    assert bsz + 1 <= mod_rows

    cc = jnp.concatenate([c, c_ctx[None, :], jnp.zeros((mod_rows - bsz - 1, d), F32)], axis=0)
    mod, w_in_b = _ada_call(cc, ada_w[layer], ada_b[layer][None, :], w_in[layer])
    mod3 = mod.reshape(mod_rows, 1, 6 * d)

    g1 = norm1_g[layer][None, :]
    cos, sin = _rope_tables(seq)
    in_width = w_in_b.shape[1]
    attn_width = (in_width - POOL_WIDTH) // 3
    z_lat = _inproj_call(x, g1, mod3, lambda b: b, w_in_b, cos, sin,
                         rope=True, col0=0, ncols=in_width)
    kv_ctx = _inproj_call(ctx, g1, mod3, lambda b: bsz, w_in_b, cos, sin,
                          rope=False, col0=POOL_WIDTH + attn_width, ncols=2 * attn_width)

    attn, u_b, v_t, wq_t, w_out_b = _attn_call(
        z_lat, kv_ctx, diff_lambda[layer], subln_g[layer][None, :],
        peer_u[layer], peer_v[layer], peer_wq[layer], w_out[layer], lam_init=lam_init)
    pool_y = _pool_call(z_lat, pool_w[layer], pool_b[layer], pool_scale[layer])
    x1 = _outproj_call(x, pool_y, attn, w_out_b, mod3)

    keys = peer_keys[layer].reshape(2 * PEER_HEADS, PEER_NKEYS, PEER_HALF)
    f, na, ea, rb, eb = _peer_sel_call(x1, norm2_g[layer][None, :], mod3, wq_t, keys)
    return _peer_call(f, u_b, v_t, na, ea, rb, eb, x1, mod3, final_g[None, :])
```
